```python
import jax, jax.numpy as jnp
from jax import lax
import numpy as np

D_MODEL = 1024
BATCH = 32
SEQ = 2048
DEPTH = 4

N_MIXERS = 3
NORM_EPS = 1e-5
D_FF = 4 * D_MODEL
NEG_INF = -1e30
FORCE = 1e6

SG_CHUNK = 128
SG_WIDTH = 2 * D_MODEL
SG_GROUPS = 8

SSM_D_INNER = 2 * D_MODEL
SSM_HEAD_DIM = 64
SSM_HEADS = SSM_D_INNER // SSM_HEAD_DIM
SSM_GROUPS = 8
SSM_STATE = 128
SSM_CONV = 4
SSM_CHUNK = 128
SSM_CONV_CH = SSM_D_INNER + 2 * SSM_GROUPS * SSM_STATE
SSM_IN = 2 * SSM_D_INNER + 2 * SSM_GROUPS * SSM_STATE + SSM_HEADS

NSA_HEADS = 16
NSA_KV_HEADS = 4
NSA_HEAD_DIM = D_MODEL // NSA_HEADS
CMP_BLOCK = 32
CMP_STRIDE = 16
CMP_HIDDEN = 4 * NSA_HEAD_DIM
SEL_BLOCK = 64
SEL_TOP_N = 16
WINDOW = 512
NSA_QBLOCK = 32
NSA_IN = NSA_HEADS * NSA_HEAD_DIM + 6 * NSA_KV_HEADS * NSA_HEAD_DIM + 3 * NSA_HEADS

kernel_name = "hybrid_gmlp_ssd_nsa_trunk"


def _rmsnorm(x, g):
    xf = x.astype(jnp.float32)
    y = xf * lax.rsqrt(jnp.mean(jnp.square(xf), axis=-1, keepdims=True) + NORM_EPS)
    return (y * g).astype(x.dtype)


def _layernorm(x, g, b):
    xf = x.astype(jnp.float32)
    mu = jnp.mean(xf, axis=-1, keepdims=True)
    var = jnp.mean(jnp.square(xf - mu), axis=-1, keepdims=True)
    return ((xf - mu) * lax.rsqrt(var + NORM_EPS) * g + b).astype(x.dtype)


def _grouped_rmsnorm(y, g, groups):
    b, s, d = y.shape
    yf = y.astype(jnp.float32).reshape(b, s, groups, d // groups)
    yf = yf * lax.rsqrt(jnp.mean(jnp.square(yf), axis=-1, keepdims=True) + NORM_EPS)
    return (yf.reshape(b, s, d) * g).astype(y.dtype)


def _sqrelu_mlp(h, w1, w2):
    return jnp.square(jax.nn.relu(h @ w1)) @ w2


def spatial_gating_mixer(h, w_in, vnorm_g, vnorm_b, w_s, b_s, w_out):
    b, s, _ = h.shape
    uv = jax.nn.gelu(h @ w_in, approximate=False)
    u, v = jnp.split(uv, 2, axis=-1)
    v = _layernorm(v, vnorm_g, vnorm_b)
    nc = s // SG_CHUNK
    v = v.reshape(b, nc, SG_CHUNK, SG_GROUPS, SG_WIDTH // SG_GROUPS)
    causal = jnp.tril(jnp.ones((SG_CHUNK, SG_CHUNK), dtype=bool))
    w = jnp.where(causal[None], w_s, jnp.zeros((), w_s.dtype))
    v = jnp.einsum("gts,bcsgd->bctgd", w, v) + b_s.T[None, None, :, :, None]
    return (u * v.reshape(b, s, SG_WIDTH)) @ w_out


def _causal_depthwise_conv(x, w, bias):
    k = w.shape[0]
    y = lax.conv_general_dilated(x, w[:, None, :], window_strides=(1,), padding=[(k - 1, 0)],
                                 dimension_numbers=("NWC", "WIO", "NWC"),
                                 feature_group_count=x.shape[-1])
    return y + bias


def _ssd_chunked(x, dt, a, bm, cm):
    f32 = jnp.float32
    b, s, h, p = x.shape
    g, n = bm.shape[2], bm.shape[3]
    r = h // g
    L = SSM_CHUNK
    c = s // L
    xdt = (x.astype(f32) * dt[..., None]).reshape(b, c, L, g, r, p)
    da_cs = jnp.cumsum((dt * a).reshape(b, c, L, g, r), axis=2)
    bm = bm.astype(f32).reshape(b, c, L, g, n)
    cm = cm.astype(f32).reshape(b, c, L, g, n)
    causal = jnp.tril(jnp.ones((L, L), dtype=bool))[None, None, :, :, None, None]
    seg = da_cs[:, :, :, None] - da_cs[:, :, None, :]
    decay = jnp.exp(jnp.where(causal, seg, -jnp.inf))
    cb = jnp.einsum("bclgn,bcsgn->bclsg", cm, bm)
    y_diag = jnp.einsum("bclsgr,bcsgrp->bclgrp", cb[..., None] * decay, xdt)
    decay_states = jnp.exp(da_cs[:, :, -1:] - da_cs)
    states = jnp.einsum("bclgn,bclgrp->bcgrpn", bm, xdt * decay_states[..., None])
    chunk_decay = jnp.exp(da_cs[:, :, -1])

    def step(carry, inp):
        st, dec = inp
        return carry * dec[..., None, None] + st, carry

    init = jnp.zeros((b, g, r, p, n), f32)
    _, prev = lax.scan(step, init, (jnp.moveaxis(states, 1, 0), jnp.moveaxis(chunk_decay, 1, 0)))
    prev = jnp.moveaxis(prev, 0, 1)
    y_off = jnp.einsum("bclgn,bcgrpn->bclgrp", cm, prev) * jnp.exp(da_cs)[..., None]
    return (y_diag + y_off).reshape(b, s, h, p).astype(x.dtype)


def mamba2_mixer(h, w_in, conv_w, conv_b, dt_bias, a_log, d_skip, norm_g, w_out):
    b, s, _ = h.shape
    gn = SSM_GROUPS * SSM_STATE
    z, xbc, dt = jnp.split(h @ w_in, [SSM_D_INNER, SSM_D_INNER + SSM_CONV_CH], axis=-1)
    xbc = jax.nn.silu(_causal_depthwise_conv(xbc, conv_w, conv_b))
    xs, bm, cm = jnp.split(xbc, [SSM_D_INNER, SSM_D_INNER + gn], axis=-1)
    dt = jax.nn.softplus(dt.astype(jnp.float32) + dt_bias.astype(jnp.float32))
    a = -jnp.exp(a_log.astype(jnp.float32))
    xs = xs.reshape(b, s, SSM_HEADS, SSM_HEAD_DIM)
    bm = bm.reshape(b, s, SSM_GROUPS, SSM_STATE)
    cm = cm.reshape(b, s, SSM_GROUPS, SSM_STATE)
    y = _ssd_chunked(xs, dt, a, bm, cm) + d_skip[:, None] * xs
    y = y.reshape(b, s, SSM_D_INNER)
    y = _grouped_rmsnorm(y * jax.nn.silu(z), norm_g, SSM_GROUPS)
    return y @ w_out


def _compress(kk, tok, pos, w1, w2):
    b, _, g, dh = kk.shape
    nc = tok.shape[0]
    blk = kk[:, tok] + pos[None, None, :, None, :]
    blk = jnp.moveaxis(blk, 3, 2).reshape(b, nc, g, CMP_BLOCK * dh)
    return jax.nn.gelu(blk @ w1, approximate=False) @ w2


def _masked_softmax(scores, mask):
    return jax.nn.softmax(jnp.where(mask, scores, NEG_INF), axis=-1)


def nsa_mixer(h, w_in, cmp_pos_k, cmp_pos_v, cmp_k_w1, cmp_k_w2, cmp_v_w1, cmp_v_w2, w_out):
    f32 = jnp.float32
    b, s, _ = h.shape
    H, G, dh = NSA_HEADS, NSA_KV_HEADS, NSA_HEAD_DIM
    r = H // G
    sizes = [H * dh] + [G * dh] * 6
    cuts = [int(c) for c in np.cumsum(sizes)]
    q, kc, vc, ks, vs, kw, vw, gl = jnp.split(h @ w_in, cuts, axis=-1)
    q = q.reshape(b, s, G, r, dh) * (dh ** -0.5)
    kc, vc, ks, vs, kw, vw = [t.reshape(b, s, G, dh) for t in (kc, vc, ks, vs, kw, vw)]
    gates = jax.nn.sigmoid(gl.astype(f32)).reshape(b, s, G, r, 3)

    nc = (s - CMP_BLOCK) // CMP_STRIDE + 1
    cmp_start = jnp.arange(nc) * CMP_STRIDE
    cmp_end = cmp_start + CMP_BLOCK - 1
    tok = cmp_start[:, None] + jnp.arange(CMP_BLOCK)[None, :]
    k_cmp = _compress(kc, tok, cmp_pos_k, cmp_k_w1, cmp_k_w2)
    v_cmp = _compress(vc, tok, cmp_pos_v, cmp_v_w1, cmp_v_w2)

    nsel = s // SEL_BLOCK
    sel_start = jnp.arange(nsel) * SEL_BLOCK
    overlap = ((cmp_start[:, None] <= sel_start[None, :] + SEL_BLOCK - 1)
               & (cmp_end[:, None] >= sel_start[None, :])).astype(f32)
    k_top = min(SEL_TOP_N, nsel)
    ks_blk = jnp.moveaxis(ks.reshape(b, nsel, SEL_BLOCK, G, dh), 3, 1)
    vs_blk = jnp.moveaxis(vs.reshape(b, nsel, SEL_BLOCK, G, dh), 3, 1)
    bi = jnp.arange(b)[:, None, None, None]
    gi = jnp.arange(G)[None, :, None, None]

    kw_pad = jnp.pad(kw, ((0, 0), (WINDOW, 0), (0, 0), (0, 0)))
    vw_pad = jnp.pad(vw, ((0, 0), (WINDOW, 0), (0, 0), (0, 0)))

    QB = NSA_QBLOCK
    nqb = s // QB
    q_blocks = jnp.moveaxis(q.reshape(b, nqb, QB, G, r, dh), 1, 0)
    g_blocks = jnp.moveaxis(gates.reshape(b, nqb, QB, G, r, 3), 1, 0)

    def attend_block(inp):
        qi, q_blk, g_blk = inp
        t = qi * QB + jnp.arange(QB)
        valid_c = cmp_end[None, :] <= t[:, None]
        sc = jnp.einsum("bqgrd,bngd->bgrqn", q_blk, k_cmp).astype(f32)
        p_c = _masked_softmax(sc, valid_c) * jnp.any(valid_c, axis=-1)[:, None].astype(f32)
        o_c = jnp.einsum("bgrqn,bngd->bqgrd", p_c.astype(v_cmp.dtype), v_cmp)
        imp = jnp.einsum("bgrqn,nj->bgqj", p_c, overlap)
        j = jnp.arange(nsel)[None, :]
        cur = (t // SEL_BLOCK)[:, None]
        forced = (j == 0) | (j == cur) | (j == cur - 1)
        future = sel_start[None, :] > t[:, None]
        imp = jnp.where(forced, FORCE, jnp.where(future, -FORCE, imp))
        _, idx = lax.top_k(imp, k_top)
        kg = ks_blk[bi, gi, idx]
        vg = vs_blk[bi, gi, idx]
        tokpos = idx[..., None] * SEL_BLOCK + jnp.arange(SEL_BLOCK)
        valid_s = (tokpos <= t[None, None, :, None, None])[:, :, None]
        ss = jnp.einsum("bqgrd,bgqksd->bgrqks", q_blk, kg).astype(f32)
        ss = jnp.where(valid_s, ss, NEG_INF).reshape(b, G, r, QB, k_top * SEL_BLOCK)
        ps = jax.nn.softmax(ss, axis=-1).reshape(b, G, r, QB, k_top, SEL_BLOCK)
        o_s = jnp.einsum("bgrqks,bgqksd->bqgrd", ps.astype(vg.dtype), vg)
        kwin = lax.dynamic_slice_in_dim(kw_pad, qi * QB, WINDOW + QB, axis=1)
        vwin = lax.dynamic_slice_in_dim(vw_pad, qi * QB, WINDOW + QB, axis=1)
        kp = qi * QB - WINDOW + jnp.arange(WINDOW + QB)
        valid_w = (kp[None, :] >= 0) & (kp[None, :] <= t[:, None]) & (kp[None, :] > t[:, None] - WINDOW)
        sw = jnp.einsum("bqgrd,bkgd->bgrqk", q_blk, kwin).astype(f32)
        pw = _masked_softmax(sw, valid_w)
        o_w = jnp.einsum("bgrqk,bkgd->bqgrd", pw.astype(vwin.dtype), vwin)
        o = g_blk[..., 0:1] * o_c + g_blk[..., 1:2] * o_s + g_blk[..., 2:3] * o_w
        return o.astype(q_blk.dtype)

    o = lax.map(attend_block, (jnp.arange(nqb), q_blocks, g_blocks))
    o = jnp.moveaxis(o, 0, 1).reshape(b, s, H * dh)
    return o @ w_out


def _normal(key, shape, scale):
    return jax.random.normal(key, shape, jnp.float32) * scale


def _sg_params(key, prefix):
    k = jax.random.split(key, 6)
    return {
        prefix + "sg_w_in": _normal(k[0], (D_MODEL, 2 * SG_WIDTH), D_MODEL ** -0.5),
        prefix + "sg_vnorm_g": 1.0 + _normal(k[1], (SG_WIDTH,), 0.05),
        prefix + "sg_vnorm_b": _normal(k[2], (SG_WIDTH,), 0.02),
        prefix + "sg_w_s": _normal(k[3], (SG_GROUPS, SG_CHUNK, SG_CHUNK), SG_CHUNK ** -0.5),
        prefix + "sg_b_s": 1.0 + _normal(k[4], (SG_GROUPS, SG_CHUNK), 0.1),
        prefix + "sg_w_out": _normal(k[5], (SG_WIDTH, D_MODEL), SG_WIDTH ** -0.5),
    }


def _ssm_params(key, prefix):
    k = jax.random.split(key, 8)
    dt = jnp.exp(jax.random.uniform(k[3], (SSM_HEADS,), jnp.float32,
                                    jnp.log(jnp.float32(1e-3)), jnp.log(jnp.float32(1e-1))))
    return {
        prefix + "ssm_w_in": _normal(k[0], (D_MODEL, SSM_IN), D_MODEL ** -0.5),
        prefix + "ssm_conv_w": _normal(k[1], (SSM_CONV, SSM_CONV_CH), SSM_CONV ** -0.5),
        prefix + "ssm_conv_b": _normal(k[2], (SSM_CONV_CH,), 0.02),
        prefix + "ssm_dt_bias": dt + jnp.log(-jnp.expm1(-dt)),
        prefix + "ssm_a_log": jnp.log(jax.random.uniform(k[4], (SSM_HEADS,), jnp.float32, 1.0, 16.0)),
        prefix + "ssm_d_skip": 1.0 + _normal(k[5], (SSM_HEADS,), 0.1),
        prefix + "ssm_norm_g": 1.0 + _normal(k[6], (SSM_D_INNER,), 0.05),
        prefix + "ssm_w_out": _normal(k[7], (SSM_D_INNER, D_MODEL), SSM_D_INNER ** -0.5),
    }


def _nsa_params(key, prefix):
    k = jax.random.split(key, 8)
    flat = CMP_BLOCK * NSA_HEAD_DIM
    return {
        prefix + "nsa_w_in": _normal(k[0], (D_MODEL, NSA_IN), D_MODEL ** -0.5),
        prefix + "nsa_cmp_pos_k": _normal(k[1], (CMP_BLOCK, NSA_HEAD_DIM), 0.1),
        prefix + "nsa_cmp_pos_v": _normal(k[2], (CMP_BLOCK, NSA_HEAD_DIM), 0.1),
        prefix + "nsa_cmp_k_w1": _normal(k[3], (flat, CMP_HIDDEN), flat ** -0.5),
        prefix + "nsa_cmp_k_w2": _normal(k[4], (CMP_HIDDEN, NSA_HEAD_DIM), CMP_HIDDEN ** -0.5),
        prefix + "nsa_cmp_v_w1": _normal(k[5], (flat, CMP_HIDDEN), flat ** -0.5),
        prefix + "nsa_cmp_v_w2": _normal(k[6], (CMP_HIDDEN, NSA_HEAD_DIM), CMP_HIDDEN ** -0.5),
        prefix + "nsa_w_out": _normal(k[7], (NSA_HEADS * NSA_HEAD_DIM, D_MODEL), (NSA_HEADS * NSA_HEAD_DIM) ** -0.5),
    }


def setup_inputs(seed: int = 0) -> dict:
    key = jax.random.key(seed)
    keys = jax.random.split(key, 6)
    params = {
        "x": jax.random.normal(keys[0], (BATCH, SEQ, D_MODEL), jnp.float32),
        "norm_gains": 1.0 + _normal(keys[1], (DEPTH, 2, D_MODEL), 0.05),
        "final_norm": 1.0 + _normal(keys[2], (D_MODEL,), 0.05),
        "ff_w1": _normal(keys[3], (DEPTH, D_MODEL, D_FF), D_MODEL ** -0.5),
        "ff_w2": _normal(keys[4], (DEPTH, D_FF, D_MODEL), D_FF ** -0.5),
    }
    builders = (_sg_params, _ssm_params, _nsa_params)
    layer_keys = jax.random.split(keys[5], DEPTH)
    for i in range(DEPTH):
        params.update(builders[i % N_MIXERS](layer_keys[i], "l%d_" % i))
    return params


def reference(x, norm_gains, final_norm, ff_w1, ff_w2,
              l0_sg_w_in, l0_sg_vnorm_g, l0_sg_vnorm_b, l0_sg_w_s, l0_sg_b_s, l0_sg_w_out,
              l1_ssm_w_in, l1_ssm_conv_w, l1_ssm_conv_b, l1_ssm_dt_bias, l1_ssm_a_log,
              l1_ssm_d_skip, l1_ssm_norm_g, l1_ssm_w_out,
              l2_nsa_w_in, l2_nsa_cmp_pos_k, l2_nsa_cmp_pos_v, l2_nsa_cmp_k_w1, l2_nsa_cmp_k_w2,
              l2_nsa_cmp_v_w1, l2_nsa_cmp_v_w2, l2_nsa_w_out,
              l3_sg_w_in, l3_sg_vnorm_g, l3_sg_vnorm_b, l3_sg_w_s, l3_sg_b_s, l3_sg_w_out):
    mixers = (spatial_gating_mixer, mamba2_mixer, nsa_mixer)
    layer_params = (
        (l0_sg_w_in, l0_sg_vnorm_g, l0_sg_vnorm_b, l0_sg_w_s, l0_sg_b_s, l0_sg_w_out),
        (l1_ssm_w_in, l1_ssm_conv_w, l1_ssm_conv_b, l1_ssm_dt_bias, l1_ssm_a_log,
         l1_ssm_d_skip, l1_ssm_norm_g, l1_ssm_w_out),
        (l2_nsa_w_in, l2_nsa_cmp_pos_k, l2_nsa_cmp_pos_v, l2_nsa_cmp_k_w1, l2_nsa_cmp_k_w2,
         l2_nsa_cmp_v_w1, l2_nsa_cmp_v_w2, l2_nsa_w_out),
        (l3_sg_w_in, l3_sg_vnorm_g, l3_sg_vnorm_b, l3_sg_w_s, l3_sg_b_s, l3_sg_w_out),
    )
    for i in range(DEPTH):
        h = _rmsnorm(x, norm_gains[i, 0])
        x = x + mixers[i % N_MIXERS](h, *layer_params[i])
        h = _rmsnorm(x, norm_gains[i, 1])
        x = x + _sqrelu_mlp(h, ff_w1[i], ff_w2[i])
    return _rmsnorm(x, final_norm)
```

```python
import functools

import numpy as np
import jax
import jax.numpy as jnp
from jax import lax
from jax.experimental import pallas as pl
from jax.experimental.pallas import tpu as pltpu

F32 = jnp.float32
BF16 = jnp.bfloat16

D_MODEL = 1024
D_FF = 4 * D_MODEL
NORM_EPS = 1e-5
NEG_INF = -1e30
FORCE = 1e6
SQRT_HALF = 0.7071067811865476

SG_CHUNK = 128
SG_WIDTH = 2 * D_MODEL
SG_GROUPS = 8
SG_GW = SG_WIDTH // SG_GROUPS

SSM_D_INNER = 2 * D_MODEL
SSM_HEAD_DIM = 64
SSM_HEADS = SSM_D_INNER // SSM_HEAD_DIM
SSM_GROUPS = 8
SSM_STATE = 128
SSM_CONV = 4
SSM_CHUNK = 128
SSM_GN = SSM_GROUPS * SSM_STATE
SSM_CONV_CH = SSM_D_INNER + 2 * SSM_GN
SSM_GW = SSM_D_INNER // SSM_GROUPS
SSM_HPG = SSM_HEADS // SSM_GROUPS

NSA_HEADS = 16
NSA_KV = 4
NSA_DH = 64
NSA_REP = NSA_HEADS // NSA_KV
CMP_BLOCK = 32
CMP_STRIDE = 16
CMP_HIDDEN = 4 * NSA_DH
SEL_BLOCK = 64
SEL_TOP_N = 16
WINDOW = 512
NSA_TQ = 128
NSA_TK = 256
NSA_WSPAN = WINDOW + NSA_TQ

LANE = 128
VMEM_LIMIT = 56 * 1024 * 1024


def _params(sem):
    return pltpu.CompilerParams(dimension_semantics=sem, vmem_limit_bytes=VMEM_LIMIT)


def _resident(shape):
    nd = len(shape)
    return pl.BlockSpec(shape, lambda *_: (0,) * nd, pipeline_mode=pl.Buffered(1))


def _rms(x, g):
    return x * lax.rsqrt(jnp.mean(x * x, axis=-1, keepdims=True) + NORM_EPS) * g


def _gelu(a):
    return 0.5 * a * (1.0 + lax.erf(a * SQRT_HALF))


def _silu(a):
    return a * jax.nn.sigmoid(a)


def _dot(a, b):
    return jnp.dot(a, b, preferred_element_type=F32)


def _dot_nt(a, b):
    return lax.dot_general(a, b, (((1,), (1,)), ((), ())), preferred_element_type=F32)


def _dot_tn(a, b):
    return lax.dot_general(a, b, (((0,), (0,)), ((), ())), preferred_element_type=F32)


def _split3(v):
    hi = v.astype(BF16)
    r = v - hi.astype(F32)
    mid = r.astype(BF16)
    lo = (r - mid.astype(F32)).astype(BF16)
    return hi, mid, lo


def _norm_proj_kernel(x_ref, g_ref, w_ref, *out_refs, outs, tn):
    h = _rms(x_ref[...], g_ref[...]).astype(BF16)
    c0 = 0
    for o_ref, (width, _, scale) in zip(out_refs, outs):
        for c in range(0, width, tn):
            ce = min(c + tn, width)
            a = _dot(h, w_ref[:, c0 + c:c0 + ce])
            if scale != 1.0:
                a = a * scale
            o_ref[:, c:ce] = a.astype(o_ref.dtype)
        c0 += width


def _norm_proj(x2, gain, w, outs, tm=512, tn=512):
    n, d = x2.shape
    wtot = sum(o[0] for o in outs)
    assert w.shape == (d, wtot)
    return pl.pallas_call(
        functools.partial(_norm_proj_kernel, outs=tuple(outs), tn=tn),
        grid=(n // tm,),
        in_specs=[pl.BlockSpec((tm, d), lambda i: (i, 0)),
                  _resident((1, d)),
                  _resident((d, wtot))],
        out_specs=[pl.BlockSpec((tm, o[0]), lambda i: (i, 0)) for o in outs],
        out_shape=[jax.ShapeDtypeStruct((n, o[0]), o[1]) for o in outs],
        compiler_params=_params(("parallel",)),
        name="norm_proj",
    )(x2, gain.reshape(1, d), w)


def _proj_res_kernel(a_ref, w_ref, x_ref, o_ref):
    o_ref[...] = x_ref[...] + _dot(a_ref[...], w_ref[...])


def _proj_residual(a, w, x2, tm=512):
    n, k = a.shape
    d = x2.shape[1]
    return pl.pallas_call(
        _proj_res_kernel,
        grid=(n // tm,),
        in_specs=[pl.BlockSpec((tm, k), lambda i: (i, 0)),
                  _resident((k, d)),
                  pl.BlockSpec((tm, d), lambda i: (i, 0))],
        out_specs=pl.BlockSpec((tm, d), lambda i: (i, 0)),
        out_shape=jax.ShapeDtypeStruct((n, d), F32),
        compiler_params=_params(("parallel",)),
        name="proj_residual",
    )(a, w, x2)


def _mlp_kernel(x_ref, g_ref, w1_ref, w2_ref, fg_ref, o_ref, h_ref, acc_ref, *, tf, final):
    x = x_ref[...]
    h_ref[...] = _rms(x, g_ref[...]).astype(BF16)
    acc_ref[...] = x
    for c in range(0, D_FF, tf):
        a = _dot(h_ref[...], w1_ref[:, c:c + tf])
        a = jnp.square(jnp.maximum(a, 0.0)).astype(BF16)
        acc_ref[...] += _dot(a, w2_ref[c:c + tf, :])
    y = acc_ref[...]
    if final:
        y = _rms(y, fg_ref[...])
    o_ref[...] = y


def _mlp(x2, gain, w1, w2, final_gain=None, tm=512, tf=512):
    n, d = x2.shape
    final = final_gain is not None
    fg = (final_gain if final else gain).reshape(1, d)
    return pl.pallas_call(
        functools.partial(_mlp_kernel, tf=tf, final=final),
        grid=(n // tm,),
        in_specs=[pl.BlockSpec((tm, d), lambda i: (i, 0)),
                  _resident((1, d)),
                  _resident((d, D_FF)),
                  _resident((D_FF, d)),
                  _resident((1, d))],
        out_specs=pl.BlockSpec((tm, d), lambda i: (i, 0)),
        out_shape=jax.ShapeDtypeStruct((n, d), F32),
        scratch_shapes=[pltpu.VMEM((tm, d), BF16), pltpu.VMEM((tm, d), F32)],
        compiler_params=_params(("parallel",)),
        name="sqrelu_mlp",
    )(x2, gain.reshape(1, d), w1, w2, fg)


def _gmlp_kernel(x_ref, g_ref, win_ref, lng_ref, lnb_ref, ws_ref, bs_ref, wout_ref, o_ref,
                 h_ref, u_ref, v_ref, vb_ref, gt_ref, *, tm, tn):
    x = x_ref[...]
    h_ref[...] = _rms(x, g_ref[...]).astype(BF16)
    for c in range(0, 2 * SG_WIDTH, tn):
        a = _gelu(_dot(h_ref[...], win_ref[:, c:c + tn]))
        if c < SG_WIDTH:
            u_ref[:, c:c + tn] = a.astype(BF16)
        else:
            v_ref[:, c - SG_WIDTH:c - SG_WIDTH + tn] = a
    v = v_ref[...]
    mu = jnp.mean(v, axis=-1, keepdims=True)
    dv = v - mu
    var = jnp.mean(dv * dv, axis=-1, keepdims=True)
    vb_ref[...] = (dv * lax.rsqrt(var + NORM_EPS) * lng_ref[...] + lnb_ref[...]).astype(BF16)
    row = lax.broadcasted_iota(jnp.int32, (SG_CHUNK, SG_CHUNK), 0)
    col = lax.broadcasted_iota(jnp.int32, (SG_CHUNK, SG_CHUNK), 1)
    causal = col <= row
    for g in range(SG_GROUPS):
        wg = jnp.where(causal, ws_ref[g], 0.0).astype(BF16)
        bg = bs_ref[:, g:g + 1]
        cs = slice(g * SG_GW, (g + 1) * SG_GW)
        for r0 in range(0, tm, SG_CHUNK):
            rs = slice(r0, r0 + SG_CHUNK)
            sv = _dot(wg, vb_ref[rs, cs]) + bg
            gt_ref[rs, cs] = (u_ref[rs, cs].astype(F32) * sv).astype(BF16)
    o_ref[...] = x + _dot(gt_ref[...], wout_ref[...])


def _gmlp_layer(x2, gain, w_in, vnorm_g, vnorm_b, w_s, b_s, w_out, tm=256, tn=512):
    n, d = x2.shape
    return pl.pallas_call(
        functools.partial(_gmlp_kernel, tm=tm, tn=tn),
        grid=(n // tm,),
        in_specs=[pl.BlockSpec((tm, d), lambda i: (i, 0)),
                  _resident((1, d)),
                  _resident((d, 2 * SG_WIDTH)),
                  _resident((1, SG_WIDTH)),
                  _resident((1, SG_WIDTH)),
                  _resident((SG_GROUPS, SG_CHUNK, SG_CHUNK)),
                  _resident((SG_CHUNK, SG_GROUPS)),
                  _resident((SG_WIDTH, d))],
        out_specs=pl.BlockSpec((tm, d), lambda i: (i, 0)),
        out_shape=jax.ShapeDtypeStruct((n, d), F32),
        scratch_shapes=[pltpu.VMEM((tm, d), BF16),
                        pltpu.VMEM((tm, SG_WIDTH), BF16),
                        pltpu.VMEM((tm, SG_WIDTH), F32),
                        pltpu.VMEM((tm, SG_WIDTH), BF16),
                        pltpu.VMEM((tm, SG_WIDTH), BF16)],
        compiler_params=_params(("parallel",)),
        name="gmlp_mixer",
    )(x2, gain.reshape(1, d), w_in.astype(BF16), vnorm_g.reshape(1, -1), vnorm_b.reshape(1, -1),
      w_s, b_s.T, w_out.astype(BF16))


def _ssd_kernel(z_ref, xbc_ref, dt_ref, cw_ref, cb_ref, dtb_ref, alog_ref, dsk_ref, ng_ref,
                tril_ref, e_ref, o_ref, xpad_ref, st_ref, xs_ref, b_ref, c_ref, *, cw):
    L = SSM_CHUNK
    hist = 8

    @pl.when(pl.program_id(1) == 0)
    def _():
        xpad_ref[0:hist, :] = jnp.zeros((hist, SSM_CONV_CH), F32)
        st_ref[...] = jnp.zeros_like(st_ref)

    for j in range(0, SSM_CONV_CH, cw):
        js = slice(j, j + cw)
        xpad_ref[hist:hist + L, js] = xbc_ref[:, js].astype(F32)
        acc = cb_ref[:, js] + cw_ref[SSM_CONV - 1:SSM_CONV, js] * xpad_ref[hist:hist + L, js]
        for k in range(1, SSM_CONV):
            acc = acc + cw_ref[SSM_CONV - 1 - k:SSM_CONV - k, js] * xpad_ref[hist - k:hist - k + L, js]
        xpad_ref[0:hist, js] = xpad_ref[L:L + hist, js]
        act = _silu(acc)
        if j < SSM_D_INNER:
            xs_ref[:, js] = act
        elif j < SSM_D_INNER + SSM_GN:
            b_ref[:, j - SSM_D_INNER:j - SSM_D_INNER + cw] = act.astype(BF16)
        else:
            o0 = j - SSM_D_INNER - SSM_GN
            c_ref[:, o0:o0 + cw] = act.astype(BF16)

    dtr = dt_ref[...] + dtb_ref[...]
    dt = jnp.maximum(dtr, 0.0) + jnp.log1p(jnp.exp(-jnp.abs(dtr)))
    da = dt * (-jnp.exp(alog_ref[...]))
    tril = tril_ref[...]
    cs = sum(_dot(tril, p) for p in _split3(da))
    cs_t = cs.T
    dt3 = _split3(dt)
    cs3 = _split3(cs)
    row = lax.broadcasted_iota(jnp.int32, (L, L), 0)
    col = lax.broadcasted_iota(jnp.int32, (L, L), 1)
    causal = col <= row

    for g in range(SSM_GROUPS):
        sl = slice(g * SSM_GW, (g + 1) * SSM_GW)
        eg = e_ref[:, sl]
        dt_e = sum(_dot(p, eg) for p in dt3)
        cs_e = sum(_dot(p, eg) for p in cs3)
        xs_g = xs_ref[:, sl]
        xdt = xs_g * dt_e
        xdt_b = xdt.astype(BF16)
        cs_last = cs_e[L - 1:L, :]
        xw_b = (xdt * jnp.exp(cs_last - cs_e)).astype(BF16)
        bg = b_ref[:, g * SSM_STATE:(g + 1) * SSM_STATE]
        cg = c_ref[:, g * SSM_STATE:(g + 1) * SSM_STATE]
        cb = _dot_nt(cg, bg)
        st_prev = st_ref[g]
        y_off = _dot(cg, st_prev.astype(BF16)) * jnp.exp(cs_e)
        st_ref[g] = st_prev * jnp.exp(cs_last) + _dot_tn(bg, xw_b)
        ys = []
        for r in range(SSM_HPG):
            h = g * SSM_HPG + r
            seg = cs[:, h:h + 1] - cs_t[h:h + 1, :]
            dec = jnp.exp(jnp.where(causal, seg, NEG_INF))
            m = (cb * dec).astype(BF16)
            ys.append(_dot(m, xdt_b[:, r * SSM_HEAD_DIM:(r + 1) * SSM_HEAD_DIM]))
        y = jnp.concatenate(ys, axis=1) + y_off + dsk_ref[:, sl] * xs_g
        yz = y * _silu(z_ref[:, sl].astype(F32))
        ms = jnp.mean(yz * yz, axis=-1, keepdims=True)
        o_ref[:, sl] = (yz * lax.rsqrt(ms + NORM_EPS) * ng_ref[:, sl]).astype(BF16)


def _ssd_core(z, xbc, dtp, conv_w, conv_b, dt_bias, a_log, d_skip, norm_g, batch, seq):
    L = SSM_CHUNK
    nc = seq // L
    tril = jnp.asarray(np.tril(np.ones((L, L), np.float32)), BF16)
    e_np = np.zeros((LANE, SSM_D_INNER), np.float32)
    for h in range(SSM_HEADS):
        e_np[h, h * SSM_HEAD_DIM:(h + 1) * SSM_HEAD_DIM] = 1.0
    expand = jnp.asarray(e_np, BF16)
    pad = LANE - SSM_HEADS
    dtb = jnp.pad(dt_bias.astype(F32), (0, pad)).reshape(1, LANE)
    alog = jnp.pad(a_log.astype(F32), (0, pad)).reshape(1, LANE)
    dsk = jnp.repeat(d_skip.astype(F32), SSM_HEAD_DIM).reshape(1, SSM_D_INNER)
    row = lambda b, c: (b * nc + c, 0)
    return pl.pallas_call(
        functools.partial(_ssd_kernel, cw=512),
        grid=(batch, nc),
        in_specs=[pl.BlockSpec((L, SSM_D_INNER), row),
                  pl.BlockSpec((L, SSM_CONV_CH), row),
                  pl.BlockSpec((L, LANE), row),
                  _resident((SSM_CONV, SSM_CONV_CH)),
                  _resident((1, SSM_CONV_CH)),
                  _resident((1, LANE)),
                  _resident((1, LANE)),
                  _resident((1, SSM_D_INNER)),
                  _resident((1, SSM_D_INNER)),
                  _resident((L, L)),
                  _resident((LANE, SSM_D_INNER))],
        out_specs=pl.BlockSpec((L, SSM_D_INNER), row),
        out_shape=jax.ShapeDtypeStruct((batch * seq, SSM_D_INNER), BF16),
        scratch_shapes=[pltpu.VMEM((L + 8, SSM_CONV_CH), F32),
                        pltpu.VMEM((SSM_GROUPS, SSM_STATE, SSM_GW), F32),
                        pltpu.VMEM((L, SSM_D_INNER), F32),
                        pltpu.VMEM((L, SSM_GN), BF16),
                        pltpu.VMEM((L, SSM_GN), BF16)],
        compiler_params=_params(("arbitrary", "arbitrary")),
        name="ssd_core",
    )(z, xbc, dtp, conv_w, conv_b.reshape(1, -1), dtb, alog, dsk, norm_g.reshape(1, -1), tril, expand)


def _mamba_layer(x2, gain, w_in, conv_w, conv_b, dt_bias, a_log, d_skip, norm_g, w_out, batch, seq):
    d = x2.shape[1]
    nzx = SSM_D_INNER + SSM_CONV_CH
    w_dt = jnp.pad(w_in[:, nzx:], ((0, 0), (0, LANE - SSM_HEADS)))
    w_cat = jnp.concatenate([w_in[:, :nzx], w_dt], axis=1).astype(BF16)
    z, xbc, dtp = _norm_proj(x2, gain, w_cat,
                             [(SSM_D_INNER, BF16, 1.0), (SSM_CONV_CH, BF16, 1.0), (LANE, F32, 1.0)])
    yn = _ssd_core(z, xbc, dtp, conv_w, conv_b, dt_bias, a_log, d_skip, norm_g, batch, seq)
    return _proj_residual(yn, w_out.astype(BF16), x2)


def _cmp_kernel(x_ref, w1_ref, w2_ref, pos_ref, o_ref):
    half = CMP_STRIDE * NSA_DH
    xh = x_ref[...]
    a = _dot(xh, w1_ref[0:half, :])
    b = _dot(xh, w1_ref[half:2 * half, :])
    posb = _dot(jnp.broadcast_to(pos_ref[...], (8, 2 * half)).astype(BF16), w1_ref[...])[0:1, :]
    nrow = xh.shape[0]
    pre = a + pltpu.roll(b, nrow - 1, 0) + posb
    o_ref[...] = _dot(_gelu(pre).astype(BF16), w2_ref[...]).astype(o_ref.dtype)


def _nsa_compress(kv_half, w1, w2, pos):
    two, batch, g, nh, width = kv_half.shape
    return pl.pallas_call(
        _cmp_kernel,
        grid=(two, batch, g),
        in_specs=[pl.BlockSpec((None, None, None, nh, width), lambda t, b, k: (t, b, k, 0, 0)),
                  pl.BlockSpec((None, width * 2, CMP_HIDDEN), lambda t, b, k: (t, 0, 0)),
                  pl.BlockSpec((None, CMP_HIDDEN, NSA_DH), lambda t, b, k: (t, 0, 0)),
                  pl.BlockSpec((None, 1, width * 2), lambda t, b, k: (t, 0, 0))],
        out_specs=pl.BlockSpec((None, None, None, nh, NSA_DH), lambda t, b, k: (t, b, k, 0, 0)),
        out_shape=jax.ShapeDtypeStruct((two, batch, g, nh, NSA_DH), BF16),
        compiler_params=_params(("parallel", "parallel", "parallel")),
        name="nsa_compress",
    )(kv_half, w1, w2, pos)


def _nsa_attn_kernel(q_ref, gl_ref, ks_ref, vs_ref, kw_ref, vw_ref, kc_ref, vc_ref, ovt_ref, ex_ref,
                     o_ref, bias_ref, m_ref, l_ref, acc_ref):
    R, TQ, TK, DH = NSA_REP, NSA_TQ, NSA_TK, NSA_DH
    qi = pl.program_id(2)
    t0 = qi * TQ
    q2 = q_ref[...]
    qs = jnp.concatenate([q2[:, r * DH:(r + 1) * DH] for r in range(R)], axis=0)
    ti = t0 + lax.broadcasted_iota(jnp.int32, (TQ, 1), 0)

    ncp = kc_ref.shape[0]
    sc = _dot_nt(qs, kc_ref[...]).reshape(R, TQ, ncp)
    n_i = lax.broadcasted_iota(jnp.int32, (TQ, ncp), 1)
    valid_c = ((n_i * CMP_STRIDE + (CMP_BLOCK - 1)) <= ti)[None]
    scm = jnp.where(valid_c, sc, NEG_INF)
    mc = jnp.max(scm, axis=-1, keepdims=True)
    ec = jnp.where(valid_c, jnp.exp(scm - mc), 0.0)
    lc = jnp.sum(ec, axis=-1, keepdims=True)
    p_c = ec / jnp.where(lc > 0.0, lc, 1.0)
    o_c = _dot(p_c.reshape(R * TQ, ncp).astype(BF16), vc_ref[...]).reshape(R, TQ, DH)

    psum = p_c[0]
    for r in range(1, R):
        psum = psum + p_c[r]
    ovt = ovt_ref[...]
    imp_t = sum(_dot_nt(ovt, p) for p in _split3(psum))
    nsel = imp_t.shape[0]
    j_i = lax.broadcasted_iota(jnp.int32, (nsel, TQ), 0)
    t_l = t0 + lax.broadcasted_iota(jnp.int32, (nsel, TQ), 1)
    cur = lax.shift_right_logical(t_l, 6)
    forced = (j_i == 0) | (j_i == cur) | (j_i == cur - 1)
    future = j_i * SEL_BLOCK > t_l
    val = jnp.where(forced, FORCE, jnp.where(future, -FORCE, imp_t))
    cnt = jnp.zeros((nsel, TQ), F32)
    for i in range(nsel):
        vi = val[i:i + 1, :]
        ge = jnp.where(vi >= val, 1.0, 0.0)
        gt = jnp.where(vi > val, 1.0, 0.0)
        cnt = cnt + jnp.where(j_i > i, ge, gt)
    sel_t = jnp.where(cnt < float(SEL_TOP_N), 1.0, 0.0)
    sel = jnp.concatenate([sel_t, jnp.zeros((LANE - nsel, TQ), F32)], axis=0).T
    seq = ex_ref.shape[1]
    hit = _dot(sel.astype(BF16), ex_ref[...])
    s_i = lax.broadcasted_iota(jnp.int32, (TQ, seq), 1)
    bias_ref[...] = jnp.where((hit > 0.5) & (s_i <= ti), 0.0, NEG_INF)

    m_ref[...] = jnp.full(m_ref.shape, NEG_INF, F32)
    l_ref[...] = jnp.zeros(l_ref.shape, F32)
    acc_ref[...] = jnp.zeros(acc_ref.shape, F32)

    def sel_tile(kt, carry):
        k0 = pl.multiple_of(kt * TK, TK)
        s = _dot_nt(qs, ks_ref[pl.ds(k0, TK), :]).reshape(R, TQ, TK) + bias_ref[:, pl.ds(k0, TK)][None]
        m_old = m_ref[...]
        m_new = jnp.maximum(m_old, jnp.max(s, axis=-1, keepdims=True))
        p = jnp.exp(s - m_new)
        alpha = jnp.exp(m_old - m_new)
        l_ref[...] = alpha * l_ref[...] + jnp.sum(p, axis=-1, keepdims=True)
        pv = _dot(p.reshape(R * TQ, TK).astype(BF16), vs_ref[pl.ds(k0, TK), :])
        acc_ref[...] = alpha * acc_ref[...] + pv.reshape(R, TQ, DH)
        m_ref[...] = m_new
        return carry

    lax.fori_loop(0, (t0 + TQ + TK - 1) // TK, sel_tile, 0)
    o_s = acc_ref[...] / l_ref[...]

    k0 = pl.multiple_of(jnp.maximum(t0 - WINDOW, 0), TQ)
    sw = _dot_nt(qs, kw_ref[pl.ds(k0, NSA_WSPAN), :]).reshape(R, TQ, NSA_WSPAN)
    kp = k0 + lax.broadcasted_iota(jnp.int32, (TQ, NSA_WSPAN), 1)
    valid_w = ((kp <= ti) & (kp > ti - WINDOW))[None]
    sw = jnp.where(valid_w, sw, NEG_INF)
    pw = jnp.exp(sw - jnp.max(sw, axis=-1, keepdims=True))
    lw = jnp.sum(pw, axis=-1, keepdims=True)
    o_w = _dot(pw.reshape(R * TQ, NSA_WSPAN).astype(BF16), vw_ref[pl.ds(k0, NSA_WSPAN), :])
    o_w = o_w.reshape(R, TQ, DH) / lw

    gates = jax.nn.sigmoid(gl_ref[...])
    outs = []
    for r in range(R):
        outs.append(gates[:, 3 * r:3 * r + 1] * o_c[r] + gates[:, 3 * r + 1:3 * r + 2] * o_s[r]
                    + gates[:, 3 * r + 2:3 * r + 3] * o_w[r])
    o_ref[...] = jnp.concatenate(outs, axis=1).astype(o_ref.dtype)


def _nsa_attention(q, gl, kvt, cmp, batch, seq):
    R, TQ, DH, G = NSA_REP, NSA_TQ, NSA_DH, NSA_KV
    nq = seq // TQ
    ncp = cmp.shape[3]
    nsel = seq // SEL_BLOCK
    nc = (seq - CMP_BLOCK) // CMP_STRIDE + 1
    ov = np.zeros((nsel, ncp), np.float32)
    for n in range(nc):
        for j in range(nsel):
            if n * CMP_STRIDE <= j * SEL_BLOCK + SEL_BLOCK - 1 and n * CMP_STRIDE + CMP_BLOCK - 1 >= j * SEL_BLOCK:
                ov[j, n] = 1.0
    ex = np.zeros((LANE, seq), np.float32)
    for j in range(nsel):
        ex[j, j * SEL_BLOCK:(j + 1) * SEL_BLOCK] = 1.0
    kv_spec = lambda idx: pl.BlockSpec((None, None, None, seq, DH), lambda b, g, i: (idx, b, g, 0, 0))
    cmp_spec = lambda idx: pl.BlockSpec((None, None, None, ncp, DH), lambda b, g, i: (idx, b, g, 0, 0))
    return pl.pallas_call(
        _nsa_attn_kernel,
        grid=(batch, G, nq),
        in_specs=[pl.BlockSpec((TQ, R * DH), lambda b, g, i: (b * nq + i, g)),
                  pl.BlockSpec((TQ, LANE), lambda b, g, i: (b * nq + i, g)),
                  kv_spec(2), kv_spec(3), kv_spec(4), kv_spec(5),
                  cmp_spec(0), cmp_spec(1),
                  _resident((nsel, ncp)),
                  _resident((LANE, seq))],
        out_specs=pl.BlockSpec((TQ, R * DH), lambda b, g, i: (b * nq + i, g)),
        out_shape=jax.ShapeDtypeStruct((batch * seq, NSA_HEADS * DH), BF16),
        scratch_shapes=[pltpu.VMEM((TQ, seq), F32),
                        pltpu.VMEM((R, TQ, 1), F32),
                        pltpu.VMEM((R, TQ, 1), F32),
                        pltpu.VMEM((R, TQ, DH), F32)],
        compiler_params=_params(("parallel", "parallel", "arbitrary")),
        name="nsa_attention",
    )(q, gl, kvt, kvt, kvt, kvt, cmp, cmp, jnp.asarray(ov, BF16), jnp.asarray(ex, BF16))


def _nsa_layer(x2, gain, w_in, pos_k, pos_v, k_w1, k_w2, v_w1, v_w2, w_out, batch, seq):
    d = x2.shape[1]
    nq = NSA_HEADS * NSA_DH
    nkv = 6 * NSA_KV * NSA_DH
    ngate = 3 * NSA_REP
    w_gate = w_in[:, nq + nkv:].reshape(d, NSA_KV, ngate)
    w_gate = jnp.pad(w_gate, ((0, 0), (0, 0), (0, LANE - ngate))).reshape(d, NSA_KV * LANE)
    w_cat = jnp.concatenate([w_in[:, :nq + nkv], w_gate], axis=1).astype(BF16)
    q, kv, gl = _norm_proj(x2, gain, w_cat,
                           [(nq, BF16, NSA_DH ** -0.5), (nkv, BF16, 1.0), (NSA_KV * LANE, F32, 1.0)])
    kvt = kv.reshape(batch, seq, 6, NSA_KV, NSA_DH).transpose(2, 0, 3, 1, 4)
    half = kvt[0:2].reshape(2, batch, NSA_KV, seq // CMP_STRIDE, CMP_STRIDE * NSA_DH)
    cmp = _nsa_compress(half,
                        jnp.stack([k_w1, v_w1]).astype(BF16),
                        jnp.stack([k_w2, v_w2]).astype(BF16),
                        jnp.stack([pos_k.reshape(1, -1), pos_v.reshape(1, -1)]))
    o = _nsa_attention(q, gl, kvt, cmp, batch, seq)
    return _proj_residual(o, w_out.astype(BF16), x2)


def kernel(x, norm_gains, final_norm, ff_w1, ff_w2, l0_sg_w_in, l0_sg_vnorm_g, l0_sg_vnorm_b, l0_sg_w_s, l0_sg_b_s, l0_sg_w_out, l1_ssm_w_in, l1_ssm_conv_w, l1_ssm_conv_b, l1_ssm_dt_bias, l1_ssm_a_log, l1_ssm_d_skip, l1_ssm_norm_g, l1_ssm_w_out, l2_nsa_w_in, l2_nsa_cmp_pos_k, l2_nsa_cmp_pos_v, l2_nsa_cmp_k_w1, l2_nsa_cmp_k_w2, l2_nsa_cmp_v_w1, l2_nsa_cmp_v_w2, l2_nsa_w_out, l3_sg_w_in, l3_sg_vnorm_g, l3_sg_vnorm_b, l3_sg_w_s, l3_sg_b_s, l3_sg_w_out):
    batch, seq, d = x.shape
    x2 = x.reshape(batch * seq, d)
    w1 = ff_w1.astype(BF16)
    w2 = ff_w2.astype(BF16)

    x2 = _gmlp_layer(x2, norm_gains[0, 0], l0_sg_w_in, l0_sg_vnorm_g, l0_sg_vnorm_b, l0_sg_w_s, l0_sg_b_s, l0_sg_w_out)
    x2 = _mlp(x2, norm_gains[0, 1], w1[0], w2[0])
    x2 = _mamba_layer(x2, norm_gains[1, 0], l1_ssm_w_in, l1_ssm_conv_w, l1_ssm_conv_b, l1_ssm_dt_bias,
                      l1_ssm_a_log, l1_ssm_d_skip, l1_ssm_norm_g, l1_ssm_w_out, batch, seq)
    x2 = _mlp(x2, norm_gains[1, 1], w1[1], w2[1])
    x2 = _nsa_layer(x2, norm_gains[2, 0], l2_nsa_w_in, l2_nsa_cmp_pos_k, l2_nsa_cmp_pos_v, l2_nsa_cmp_k_w1,
                    l2_nsa_cmp_k_w2, l2_nsa_cmp_v_w1, l2_nsa_cmp_v_w2, l2_nsa_w_out, batch, seq)
    x2 = _mlp(x2, norm_gains[2, 1], w1[2], w2[2])
    x2 = _gmlp_layer(x2, norm_gains[3, 0], l3_sg_w_in, l3_sg_vnorm_g, l3_sg_vnorm_b, l3_sg_w_s, l3_sg_b_s, l3_sg_w_out)
    x2 = _mlp(x2, norm_gains[3, 1], w1[3], w2[3], final_gain=final_norm)
    return x2.reshape(batch, seq, d)
```

```python
import functools

import numpy as np
import jax
import jax.numpy as jnp
from jax import lax
from jax.experimental import pallas as pl
from jax.experimental.pallas import tpu as pltpu

F32 = jnp.float32
BF16 = jnp.bfloat16

D_MODEL = 1024
D_FF = 4 * D_MODEL
NORM_EPS = 1e-5
NEG_INF = -1e30
FORCE = 1e6
SQRT_HALF = 0.7071067811865476
LOG2E = 1.4426950408889634

SG_CHUNK = 128
SG_WIDTH = 2 * D_MODEL
SG_GROUPS = 8
SG_GW = SG_WIDTH // SG_GROUPS

SSM_D_INNER = 2 * D_MODEL
SSM_HEAD_DIM = 64
SSM_HEADS = SSM_D_INNER // SSM_HEAD_DIM
SSM_GROUPS = 8
SSM_STATE = 128
SSM_CONV = 4
SSM_CHUNK = 128
SSM_GN = SSM_GROUPS * SSM_STATE
SSM_CONV_CH = SSM_D_INNER + 2 * SSM_GN
SSM_GW = SSM_D_INNER // SSM_GROUPS
SSM_HPG = SSM_HEADS // SSM_GROUPS

NSA_HEADS = 16
NSA_KV = 4
NSA_DH = 64
NSA_REP = NSA_HEADS // NSA_KV
CMP_BLOCK = 32
CMP_STRIDE = 16
CMP_HIDDEN = 4 * NSA_DH
SEL_BLOCK = 64
SEL_TOP_N = 16
WINDOW = 512
NSA_TQ = 128
NSA_TK = 256
NSA_GPS = 4
NSA_WSPAN = WINDOW + NSA_TQ

LANE = 128
VMEM_LIMIT = 56 * 1024 * 1024


def _params(sem):
    return pltpu.CompilerParams(dimension_semantics=sem, vmem_limit_bytes=VMEM_LIMIT)


def _resident(shape):
    nd = len(shape)
    return pl.BlockSpec(shape, lambda *_: (0,) * nd, pipeline_mode=pl.Buffered(1))


def _rms(x, g):
    return x * lax.rsqrt(jnp.mean(x * x, axis=-1, keepdims=True) + NORM_EPS) * g


def _gelu(a):
    return 0.5 * a * (1.0 + lax.erf(a * SQRT_HALF))


def _silu(a):
    return a * jax.nn.sigmoid(a)


def _dot(a, b):
    return jnp.dot(a, b, preferred_element_type=F32)


def _dot_nt(a, b):
    return lax.dot_general(a, b, (((1,), (1,)), ((), ())), preferred_element_type=F32)


def _dot_tn(a, b):
    return lax.dot_general(a, b, (((0,), (0,)), ((), ())), preferred_element_type=F32)


def _split3(v):
    hi = v.astype(BF16)
    r = v - hi.astype(F32)
    mid = r.astype(BF16)
    lo = (r - mid.astype(F32)).astype(BF16)
    return hi, mid, lo


def _norm_proj_kernel(x_ref, g_ref, w_ref, *out_refs, outs, tn):
    h = _rms(x_ref[...], g_ref[...]).astype(BF16)
    c0 = 0
    for o_ref, (width, _, scale) in zip(out_refs, outs):
        for c in range(0, width, tn):
            ce = min(c + tn, width)
            a = _dot(h, w_ref[:, c0 + c:c0 + ce])
            if scale != 1.0:
                a = a * scale
            o_ref[:, c:ce] = a.astype(o_ref.dtype)
        c0 += width


def _norm_proj(x2, gain, w, outs, tm=512, tn=512):
    n, d = x2.shape
    wtot = sum(o[0] for o in outs)
    assert w.shape == (d, wtot)
    return pl.pallas_call(
        functools.partial(_norm_proj_kernel, outs=tuple(outs), tn=tn),
        grid=(n // tm,),
        in_specs=[pl.BlockSpec((tm, d), lambda i: (i, 0)),
                  _resident((1, d)),
                  _resident((d, wtot))],
        out_specs=[pl.BlockSpec((tm, o[0]), lambda i: (i, 0)) for o in outs],
        out_shape=[jax.ShapeDtypeStruct((n, o[0]), o[1]) for o in outs],
        compiler_params=_params(("parallel",)),
        name="norm_proj",
    )(x2, gain.reshape(1, d), w)


def _proj_res_kernel(a_ref, w_ref, x_ref, o_ref):
    o_ref[...] = x_ref[...] + _dot(a_ref[...], w_ref[...])


def _proj_residual(a, w, x2, tm=512):
    n, k = a.shape
    d = x2.shape[1]
    return pl.pallas_call(
        _proj_res_kernel,
        grid=(n // tm,),
        in_specs=[pl.BlockSpec((tm, k), lambda i: (i, 0)),
                  _resident((k, d)),
                  pl.BlockSpec((tm, d), lambda i: (i, 0))],
        out_specs=pl.BlockSpec((tm, d), lambda i: (i, 0)),
        out_shape=jax.ShapeDtypeStruct((n, d), F32),
        compiler_params=_params(("parallel",)),
        name="proj_residual",
    )(a, w, x2)


def _mlp_kernel(x_ref, g_ref, w1_ref, w2_ref, fg_ref, o_ref, h_ref, acc_ref, *, tf, final):
    x = x_ref[...]
    h_ref[...] = _rms(x, g_ref[...]).astype(BF16)
    acc_ref[...] = x
    for c in range(0, D_FF, tf):
        a = _dot(h_ref[...], w1_ref[:, c:c + tf])
        a = jnp.square(jnp.maximum(a, 0.0)).astype(BF16)
        acc_ref[...] += _dot(a, w2_ref[c:c + tf, :])
    y = acc_ref[...]
    if final:
        y = _rms(y, fg_ref[...])
    o_ref[...] = y


def _mlp(x2, gain, w1, w2, final_gain=None, tm=512, tf=512):
    n, d = x2.shape
    final = final_gain is not None
    fg = (final_gain if final else gain).reshape(1, d)
    return pl.pallas_call(
        functools.partial(_mlp_kernel, tf=tf, final=final),
        grid=(n // tm,),
        in_specs=[pl.BlockSpec((tm, d), lambda i: (i, 0)),
                  _resident((1, d)),
                  _resident((d, D_FF)),
                  _resident((D_FF, d)),
                  _resident((1, d))],
        out_specs=pl.BlockSpec((tm, d), lambda i: (i, 0)),
        out_shape=jax.ShapeDtypeStruct((n, d), F32),
        scratch_shapes=[pltpu.VMEM((tm, d), BF16), pltpu.VMEM((tm, d), F32)],
        compiler_params=_params(("parallel",)),
        name="sqrelu_mlp",
    )(x2, gain.reshape(1, d), w1, w2, fg)


def _gmlp_kernel(x_ref, g_ref, win_ref, lng_ref, lnb_ref, ws_ref, bs_ref, wout_ref, o_ref,
                 h_ref, u_ref, v_ref, vb_ref, gt_ref, *, tm, tn):
    x = x_ref[...]
    h_ref[...] = _rms(x, g_ref[...]).astype(BF16)
    for c in range(0, 2 * SG_WIDTH, tn):
        a = _gelu(_dot(h_ref[...], win_ref[:, c:c + tn]))
        if c < SG_WIDTH:
            u_ref[:, c:c + tn] = a.astype(BF16)
        else:
            v_ref[:, c - SG_WIDTH:c - SG_WIDTH + tn] = a
    v = v_ref[...]
    mu = jnp.mean(v, axis=-1, keepdims=True)
    dv = v - mu
    var = jnp.mean(dv * dv, axis=-1, keepdims=True)
    vb_ref[...] = (dv * lax.rsqrt(var + NORM_EPS) * lng_ref[...] + lnb_ref[...]).astype(BF16)
    row = lax.broadcasted_iota(jnp.int32, (SG_CHUNK, SG_CHUNK), 0)
    col = lax.broadcasted_iota(jnp.int32, (SG_CHUNK, SG_CHUNK), 1)
    causal = col <= row
    for g in range(SG_GROUPS):
        wg = jnp.where(causal, ws_ref[g], 0.0).astype(BF16)
        bg = bs_ref[:, g:g + 1]
        cs = slice(g * SG_GW, (g + 1) * SG_GW)
        for r0 in range(0, tm, SG_CHUNK):
            rs = slice(r0, r0 + SG_CHUNK)
            sv = _dot(wg, vb_ref[rs, cs]) + bg
            gt_ref[rs, cs] = (u_ref[rs, cs].astype(F32) * sv).astype(BF16)
    o_ref[...] = x + _dot(gt_ref[...], wout_ref[...])


def _gmlp_layer(x2, gain, w_in, vnorm_g, vnorm_b, w_s, b_s, w_out, tm=512, tn=512):
    n, d = x2.shape
    return pl.pallas_call(
        functools.partial(_gmlp_kernel, tm=tm, tn=tn),
        grid=(n // tm,),
        in_specs=[pl.BlockSpec((tm, d), lambda i: (i, 0)),
                  _resident((1, d)),
                  _resident((d, 2 * SG_WIDTH)),
                  _resident((1, SG_WIDTH)),
                  _resident((1, SG_WIDTH)),
                  _resident((SG_GROUPS, SG_CHUNK, SG_CHUNK)),
                  _resident((SG_CHUNK, SG_GROUPS)),
                  _resident((SG_WIDTH, d))],
        out_specs=pl.BlockSpec((tm, d), lambda i: (i, 0)),
        out_shape=jax.ShapeDtypeStruct((n, d), F32),
        scratch_shapes=[pltpu.VMEM((tm, d), BF16),
                        pltpu.VMEM((tm, SG_WIDTH), BF16),
                        pltpu.VMEM((tm, SG_WIDTH), F32),
                        pltpu.VMEM((tm, SG_WIDTH), BF16),
                        pltpu.VMEM((tm, SG_WIDTH), BF16)],
        compiler_params=_params(("parallel",)),
        name="gmlp_mixer",
    )(x2, gain.reshape(1, d), w_in.astype(BF16), vnorm_g.reshape(1, -1), vnorm_b.reshape(1, -1),
      w_s, b_s.T, w_out.astype(BF16))


def _ssd_kernel(z_ref, xbc_ref, dt_ref, cw_ref, cb_ref, dtb_ref, alog_ref, dsk_ref, ng_ref,
                tril_ref, e_ref, sh_ref, o_ref, xpad_ref, st_ref, xs_ref, b_ref, c_ref, *, cw):
    L = SSM_CHUNK

    @pl.when(pl.program_id(1) == 0)
    def _():
        xpad_ref[0:L, :] = jnp.zeros((L, SSM_CONV_CH), BF16)
        st_ref[...] = jnp.zeros_like(st_ref)

    xpad_ref[L:2 * L, :] = xbc_ref[...]
    for j in range(0, SSM_CONV_CH, cw):
        js = slice(j, j + cw)
        shifted = _dot(sh_ref[...], xpad_ref[:, js])
        acc = cb_ref[:, js] + cw_ref[SSM_CONV - 1:SSM_CONV, js] * xbc_ref[:, js].astype(F32)
        for k in range(1, SSM_CONV):
            acc = acc + cw_ref[SSM_CONV - 1 - k:SSM_CONV - k, js] * shifted[(k - 1) * L:k * L]
        act = _silu(acc)
        if j < SSM_D_INNER:
            xs_ref[:, js] = act
        elif j < SSM_D_INNER + SSM_GN:
            b_ref[:, j - SSM_D_INNER:j - SSM_D_INNER + cw] = act.astype(BF16)
        else:
            o0 = j - SSM_D_INNER - SSM_GN
            c_ref[:, o0:o0 + cw] = act.astype(BF16)
    xpad_ref[0:L, :] = xbc_ref[...]

    dtr = dt_ref[...] + dtb_ref[...]
    dt = jnp.maximum(dtr, 0.0) + jnp.log1p(jnp.exp(-jnp.abs(dtr)))
    da = dt * (-jnp.exp(alog_ref[...]))
    tril = tril_ref[...]
    cs = sum(_dot(tril, p) for p in _split3(da))
    cs_t = cs.T
    dt3 = _split3(dt)
    cs3 = _split3(cs)
    row = lax.broadcasted_iota(jnp.int32, (L, L), 0)
    col = lax.broadcasted_iota(jnp.int32, (L, L), 1)
    causal = col <= row

    for g in range(SSM_GROUPS):
        sl = slice(g * SSM_GW, (g + 1) * SSM_GW)
        eg = e_ref[:, sl]
        dt_e = sum(_dot(p, eg) for p in dt3)
        cs_e = sum(_dot(p, eg) for p in cs3)
        xs_g = xs_ref[:, sl]
        xdt = xs_g * dt_e
        xdt_b = xdt.astype(BF16)
        cs_last = cs_e[L - 1:L, :]
        xw_b = (xdt * jnp.exp(cs_last - cs_e)).astype(BF16)
        bg = b_ref[:, g * SSM_STATE:(g + 1) * SSM_STATE]
        cg = c_ref[:, g * SSM_STATE:(g + 1) * SSM_STATE]
        cb = _dot_nt(cg, bg)
        st_prev = st_ref[g]
        y_off = _dot(cg, st_prev.astype(BF16)) * jnp.exp(cs_e)
        st_ref[g] = st_prev * jnp.exp(cs_last) + _dot_tn(bg, xw_b)
        ys = []
        for r in range(SSM_HPG):
            h = g * SSM_HPG + r
            seg = cs[:, h:h + 1] - cs_t[h:h + 1, :]
            dec = jnp.exp(jnp.where(causal, seg, NEG_INF))
            m = (cb * dec).astype(BF16)
            ys.append(_dot(m, xdt_b[:, r * SSM_HEAD_DIM:(r + 1) * SSM_HEAD_DIM]))
        y = jnp.concatenate(ys, axis=1) + y_off + dsk_ref[:, sl] * xs_g
        yz = y * _silu(z_ref[:, sl].astype(F32))
        ms = jnp.mean(yz * yz, axis=-1, keepdims=True)
        o_ref[:, sl] = (yz * lax.rsqrt(ms + NORM_EPS) * ng_ref[:, sl]).astype(BF16)


def _ssd_core(z, xbc, dtp, conv_w, conv_b, dt_bias, a_log, d_skip, norm_g, batch, seq):
    L = SSM_CHUNK
    nc = seq // L
    tril = jnp.asarray(np.tril(np.ones((L, L), np.float32)), BF16)
    e_np = np.zeros((LANE, SSM_D_INNER), np.float32)
    for h in range(SSM_HEADS):
        e_np[h, h * SSM_HEAD_DIM:(h + 1) * SSM_HEAD_DIM] = 1.0
    expand = jnp.asarray(e_np, BF16)
    sh_np = np.zeros(((SSM_CONV - 1) * L, 2 * L), np.float32)
    for k in range(1, SSM_CONV):
        for t in range(L):
            sh_np[(k - 1) * L + t, L + t - k] = 1.0
    shift = jnp.asarray(sh_np, BF16)
    pad = LANE - SSM_HEADS
    dtb = jnp.pad(dt_bias.astype(F32), (0, pad)).reshape(1, LANE)
    alog = jnp.pad(a_log.astype(F32), (0, pad)).reshape(1, LANE)
    dsk = jnp.repeat(d_skip.astype(F32), SSM_HEAD_DIM).reshape(1, SSM_D_INNER)
    row = lambda b, c: (b * nc + c, 0)
    return pl.pallas_call(
        functools.partial(_ssd_kernel, cw=512),
        grid=(batch, nc),
        in_specs=[pl.BlockSpec((L, SSM_D_INNER), row),
                  pl.BlockSpec((L, SSM_CONV_CH), row),
                  pl.BlockSpec((L, LANE), row),
                  _resident((SSM_CONV, SSM_CONV_CH)),
                  _resident((1, SSM_CONV_CH)),
                  _resident((1, LANE)),
                  _resident((1, LANE)),
                  _resident((1, SSM_D_INNER)),
                  _resident((1, SSM_D_INNER)),
                  _resident((L, L)),
                  _resident((LANE, SSM_D_INNER)),
                  _resident(((SSM_CONV - 1) * L, 2 * L))],
        out_specs=pl.BlockSpec((L, SSM_D_INNER), row),
        out_shape=jax.ShapeDtypeStruct((batch * seq, SSM_D_INNER), BF16),
        scratch_shapes=[pltpu.VMEM((2 * L, SSM_CONV_CH), BF16),
                        pltpu.VMEM((SSM_GROUPS, SSM_STATE, SSM_GW), F32),
                        pltpu.VMEM((L, SSM_D_INNER), F32),
                        pltpu.VMEM((L, SSM_GN), BF16),
                        pltpu.VMEM((L, SSM_GN), BF16)],
        compiler_params=_params(("arbitrary", "arbitrary")),
        name="ssd_core",
    )(z, xbc, dtp, conv_w, conv_b.reshape(1, -1), dtb, alog, dsk, norm_g.reshape(1, -1), tril, expand, shift)


def _mamba_layer(x2, gain, w_in, conv_w, conv_b, dt_bias, a_log, d_skip, norm_g, w_out, batch, seq):
    d = x2.shape[1]
    nzx = SSM_D_INNER + SSM_CONV_CH
    w_dt = jnp.pad(w_in[:, nzx:], ((0, 0), (0, LANE - SSM_HEADS)))
    w_cat = jnp.concatenate([w_in[:, :nzx], w_dt], axis=1).astype(BF16)
    z, xbc, dtp = _norm_proj(x2, gain, w_cat,
                             [(SSM_D_INNER, BF16, 1.0), (SSM_CONV_CH, BF16, 1.0), (LANE, F32, 1.0)])
    yn = _ssd_core(z, xbc, dtp, conv_w, conv_b, dt_bias, a_log, d_skip, norm_g, batch, seq)
    return _proj_residual(yn, w_out.astype(BF16), x2)


def _cmp_kernel(x_ref, w1_ref, w2_ref, pos_ref, o_ref):
    half = CMP_STRIDE * NSA_DH
    xh = x_ref[...]
    a = _dot(xh, w1_ref[0:half, :])
    b = _dot(xh, w1_ref[half:2 * half, :])
    posb = _dot(jnp.broadcast_to(pos_ref[...], (8, 2 * half)).astype(BF16), w1_ref[...])[0:1, :]
    nrow = xh.shape[0]
    pre = a + pltpu.roll(b, nrow - 1, 0) + posb
    o_ref[...] = _dot(_gelu(pre).astype(BF16), w2_ref[...]).astype(o_ref.dtype)


def _nsa_compress(kv_half, w1, w2, pos):
    two, batch, g, nh, width = kv_half.shape
    return pl.pallas_call(
        _cmp_kernel,
        grid=(two, batch, g),
        in_specs=[pl.BlockSpec((None, None, None, nh, width), lambda t, b, k: (t, b, k, 0, 0)),
                  pl.BlockSpec((None, width * 2, CMP_HIDDEN), lambda t, b, k: (t, 0, 0)),
                  pl.BlockSpec((None, CMP_HIDDEN, NSA_DH), lambda t, b, k: (t, 0, 0)),
                  pl.BlockSpec((None, 1, width * 2), lambda t, b, k: (t, 0, 0))],
        out_specs=pl.BlockSpec((None, None, None, nh, NSA_DH), lambda t, b, k: (t, b, k, 0, 0)),
        out_shape=jax.ShapeDtypeStruct((two, batch, g, nh, NSA_DH), BF16),
        compiler_params=_params(("parallel", "parallel", "parallel")),
        name="nsa_compress",
    )(kv_half, w1, w2, pos)


def _nsa_attn_kernel(q_ref, gl_ref, ks_ref, vs_ref, kw_ref, vw_ref, kc_ref, vc_ref, ovt_ref, ind_ref,
                     o_ref, ksp_ref, vsp_ref, vwp_ref, vcp_ref, m_ref, acc_ref, sw_ref, pw_ref):
    R, TQ, TK, DH, NG = NSA_REP, NSA_TQ, NSA_TK, NSA_DH, NSA_GPS
    qi = pl.program_id(2)
    t0 = pl.multiple_of(qi * TQ, TQ)
    seq = ks_ref.shape[1]
    ncp = kc_ref.shape[1]
    nsel = seq // SEL_BLOCK
    groups = range(NG)

    @pl.when(qi == 0)
    def _():
        def with_ones(v):
            one = jnp.where(lax.broadcasted_iota(jnp.int32, (v.shape[0], DH), 1) == 0, 1.0, 0.0)
            return jnp.concatenate([v, one.astype(BF16)], axis=1)
        for a in groups:
            ksp_ref[a] = jnp.concatenate([ks_ref[a], ind_ref[...]], axis=1)
            vsp_ref[a] = with_ones(vs_ref[a])
            vwp_ref[a] = with_ones(vw_ref[a])
            vcp_ref[a] = with_ones(vc_ref[a])

    q_heads = [[q_ref[:, (a * R + r) * DH:(a * R + r + 1) * DH] for r in range(R)] for a in groups]
    qs = [jnp.concatenate(q_heads[a], axis=0) for a in groups]
    ti = t0 + lax.broadcasted_iota(jnp.int32, (TQ, 1), 0)

    def rows(a, r):
        return a[r * TQ:(r + 1) * TQ]

    def rowmax(s):
        return jnp.broadcast_to(jnp.max(s, axis=-1, keepdims=True), (TQ, LANE))

    n_i = lax.broadcasted_iota(jnp.int32, (TQ, ncp), 1)
    valid_c = (n_i * CMP_STRIDE + (CMP_BLOCK - 1)) <= ti
    bias_c = jnp.where(valid_c, 0.0, NEG_INF)
    ovt = ovt_ref[...]
    sc = [_dot_nt(qs[a], kc_ref[a]) for a in groups]
    imp_t = [jnp.zeros((nsel, TQ), F32) for _ in groups]
    e_list = [[] for _ in groups]
    for r in range(R):
        for a in groups:
            s = rows(sc[a], r) + bias_c
            e = jnp.where(valid_c, jnp.exp2(s - rowmax(s)), 0.0)
            parts = _split3(e)
            e_list[a].append(parts[0])
            u = sum(_dot_nt(ovt, p) for p in parts)
            l_t = u[nsel:nsel + 1, :]
            imp_t[a] = imp_t[a] + u[0:nsel, :] / jnp.where(l_t > 0.0, l_t, 1.0)
    oc = [_dot(jnp.concatenate(e_list[a], axis=0), vcp_ref[a]) for a in groups]

    j_i = lax.broadcasted_iota(jnp.int32, (nsel, TQ), 0)
    t_l = t0 + lax.broadcasted_iota(jnp.int32, (nsel, TQ), 1)
    cur = lax.shift_right_logical(t_l, 6)
    forced = (j_i == 0) | (j_i == cur) | (j_i == cur - 1)
    future = j_i * SEL_BLOCK > t_l
    val = [jnp.where(forced, FORCE, jnp.where(future, -FORCE, imp_t[a])) for a in groups]
    cnt = [jnp.zeros((nsel, TQ), F32) for _ in groups]
    for i in range(nsel):
        for a in groups:
            vi = val[a][i:i + 1, :]
            ge = jnp.where(vi >= val[a], 1.0, 0.0)
            gt = jnp.where(vi > val[a], 1.0, 0.0)
            cnt[a] = cnt[a] + jnp.where(j_i > i, ge, gt)
    in_loop = j_i < qi * (TQ // SEL_BLOCK)
    qsp = []
    for a in groups:
        sb_t = jnp.where((cnt[a] < float(SEL_TOP_N)) & in_loop, 0.0, NEG_INF)
        sb = jnp.concatenate([sb_t, jnp.zeros((LANE - nsel, TQ), F32)], axis=0).T
        sbb = sb[:, :DH].astype(BF16)
        qsp.append(jnp.concatenate([jnp.concatenate([qh, sbb], axis=1) for qh in q_heads[a]], axis=0))

    row = lax.broadcasted_iota(jnp.int32, (TQ, TQ), 0)
    col = lax.broadcasted_iota(jnp.int32, (TQ, TQ), 1)
    causal_b = jnp.where(col <= row, 0.0, NEG_INF)
    sd = [_dot_nt(qs[a], ks_ref[a, pl.ds(t0, TQ), :]) for a in groups]
    for a in groups:
        p_list = []
        for r in range(R):
            s = rows(sd[a], r) + causal_b
            m = rowmax(s)
            m_ref[a, r] = m
            p_list.append(jnp.exp2(s - m).astype(BF16))
        pv = _dot(jnp.concatenate(p_list, axis=0), vsp_ref[a, pl.ds(t0, TQ), :])
        for r in range(R):
            acc_ref[a, r] = rows(pv, r)

    def sel_tile(kt, carry):
        k0 = pl.multiple_of(kt * TK, TK)
        s_all = [_dot_nt(qsp[a], ksp_ref[a, pl.ds(k0, TK), :]) for a in groups]
        for a in groups:
            p_list, alphas = [], []
            for r in range(R):
                s = rows(s_all[a], r)
                m_old = m_ref[a, r]
                m_new = jnp.maximum(m_old, rowmax(s))
                m_ref[a, r] = m_new
                p_list.append(jnp.exp2(s - jnp.concatenate([m_new] * (TK // LANE), axis=1)).astype(BF16))
                alphas.append(jnp.exp2(m_old - m_new))
            pv = _dot(jnp.concatenate(p_list, axis=0), vsp_ref[a, pl.ds(k0, TK), :])
            for r in range(R):
                acc_ref[a, r] = alphas[r] * acc_ref[a, r] + rows(pv, r)
        return carry

    lax.fori_loop(0, lax.shift_right_logical(qi + 1, 1), sel_tile, 0)

    k0w = pl.multiple_of(jnp.maximum(t0 - WINDOW, 0), TQ)
    kp = k0w + lax.broadcasted_iota(jnp.int32, (TQ, NSA_WSPAN), 1)
    bias_w = jnp.where((kp <= ti) & (kp > ti - WINDOW), 0.0, NEG_INF)
    lane_tiles = [slice(j, j + LANE) for j in range(0, NSA_WSPAN, LANE)]
    for a in groups:
        sw_ref[a] = _dot_nt(qs[a], kw_ref[a, pl.ds(k0w, NSA_WSPAN), :])
    for r in range(R):
        for a in groups:
            rs = slice(r * TQ, (r + 1) * TQ)
            m = sw_ref[a, rs, lane_tiles[0]] + bias_w[:, lane_tiles[0]]
            for lt in lane_tiles[1:]:
                m = jnp.maximum(m, sw_ref[a, rs, lt] + bias_w[:, lt])
            m = rowmax(m)
            for lt in lane_tiles:
                pw_ref[a, rs, lt] = jnp.exp2(sw_ref[a, rs, lt] + bias_w[:, lt] - m).astype(BF16)
    pw = [_dot(pw_ref[a], vwp_ref[a, pl.ds(k0w, NSA_WSPAN), :]) for a in groups]

    outs = []
    for a in groups:
        gates = jax.nn.sigmoid(gl_ref[:, a * LANE:(a + 1) * LANE])
        for r in range(R):
            a_c, a_s, a_w = rows(oc[a], r), acc_ref[a, r], rows(pw[a], r)
            l_c = a_c[:, DH:DH + 1]
            w_c = gates[:, 3 * r:3 * r + 1] / jnp.where(l_c > 0.0, l_c, 1.0)
            w_s = gates[:, 3 * r + 1:3 * r + 2] / a_s[:, DH:DH + 1]
            w_w = gates[:, 3 * r + 2:3 * r + 3] / a_w[:, DH:DH + 1]
            outs.append(w_c * a_c[:, :DH] + w_s * a_s[:, :DH] + w_w * a_w[:, :DH])
    o_ref[...] = jnp.concatenate(outs, axis=1).astype(o_ref.dtype)


def _nsa_attention(q, gl, kvt, cmp, batch, seq):
    R, TQ, DH, G, NG = NSA_REP, NSA_TQ, NSA_DH, NSA_KV, NSA_GPS
    nq = seq // TQ
    ncp = cmp.shape[3]
    nsel = seq // SEL_BLOCK
    nc = (seq - CMP_BLOCK) // CMP_STRIDE + 1
    assert ncp == LANE and nsel < DH and NSA_TK % LANE == 0 and TQ % SEL_BLOCK == 0 and G % NG == 0
    ov = np.zeros((DH, ncp), np.float32)
    for n in range(nc):
        for j in range(nsel):
            if n * CMP_STRIDE <= j * SEL_BLOCK + SEL_BLOCK - 1 and n * CMP_STRIDE + CMP_BLOCK - 1 >= j * SEL_BLOCK:
                ov[j, n] = 1.0
    ov[nsel, :] = 1.0
    ind = np.zeros((seq, DH), np.float32)
    for j in range(nsel):
        ind[j * SEL_BLOCK:(j + 1) * SEL_BLOCK, j] = 1.0
    kv_spec = lambda idx: pl.BlockSpec((None, None, NG, seq, DH), lambda b, g, i: (idx, b, g, 0, 0))
    cmp_spec = lambda idx: pl.BlockSpec((None, None, NG, ncp, DH), lambda b, g, i: (idx, b, g, 0, 0))
    return pl.pallas_call(
        _nsa_attn_kernel,
        grid=(batch, G // NG, nq),
        in_specs=[pl.BlockSpec((TQ, NG * R * DH), lambda b, g, i: (b * nq + i, g)),
                  pl.BlockSpec((TQ, NG * LANE), lambda b, g, i: (b * nq + i, g)),
                  kv_spec(2), kv_spec(3), kv_spec(4), kv_spec(5),
                  cmp_spec(0), cmp_spec(1),
                  _resident((DH, ncp)),
                  _resident((seq, DH))],
        out_specs=pl.BlockSpec((TQ, NG * R * DH), lambda b, g, i: (b * nq + i, g)),
        out_shape=jax.ShapeDtypeStruct((batch * seq, NSA_HEADS * DH), BF16),
        scratch_shapes=[pltpu.VMEM((NG, seq, 2 * DH), BF16),
                        pltpu.VMEM((NG, seq, 2 * DH), BF16),
                        pltpu.VMEM((NG, seq, 2 * DH), BF16),
                        pltpu.VMEM((NG, ncp, 2 * DH), BF16),
                        pltpu.VMEM((NG, R, TQ, LANE), F32),
                        pltpu.VMEM((NG, R, TQ, LANE), F32),
                        pltpu.VMEM((NG, R * TQ, NSA_WSPAN), F32),
                        pltpu.VMEM((NG, R * TQ, NSA_WSPAN), BF16)],
        compiler_params=_params(("arbitrary", "arbitrary", "arbitrary")),
        name="nsa_attention",
    )(q, gl, kvt, kvt, kvt, kvt, cmp, cmp, jnp.asarray(ov, BF16), jnp.asarray(ind, BF16))


def _nsa_layer(x2, gain, w_in, pos_k, pos_v, k_w1, k_w2, v_w1, v_w2, w_out, batch, seq):
    d = x2.shape[1]
    nq = NSA_HEADS * NSA_DH
    nkv = 6 * NSA_KV * NSA_DH
    ngate = 3 * NSA_REP
    w_gate = w_in[:, nq + nkv:].reshape(d, NSA_KV, ngate)
    w_gate = jnp.pad(w_gate, ((0, 0), (0, 0), (0, LANE - ngate))).reshape(d, NSA_KV * LANE)
    w_cat = jnp.concatenate([w_in[:, :nq + nkv], w_gate], axis=1).astype(BF16)
    q, kv, gl = _norm_proj(x2, gain, w_cat,
                           [(nq, BF16, NSA_DH ** -0.5 * LOG2E), (nkv, BF16, 1.0), (NSA_KV * LANE, F32, 1.0)])
    kvt = kv.reshape(batch, seq, 6, NSA_KV, NSA_DH).transpose(2, 0, 3, 1, 4)
    half = kvt[0:2].reshape(2, batch, NSA_KV, seq // CMP_STRIDE, CMP_STRIDE * NSA_DH)
    cmp = _nsa_compress(half,
                        jnp.stack([k_w1, v_w1]).astype(BF16),
                        jnp.stack([k_w2, v_w2]).astype(BF16),
                        jnp.stack([pos_k.reshape(1, -1), pos_v.reshape(1, -1)]))
    o = _nsa_attention(q, gl, kvt, cmp, batch, seq)
    return _proj_residual(o, w_out.astype(BF16), x2)


def kernel(x, norm_gains, final_norm, ff_w1, ff_w2, l0_sg_w_in, l0_sg_vnorm_g, l0_sg_vnorm_b, l0_sg_w_s, l0_sg_b_s, l0_sg_w_out, l1_ssm_w_in, l1_ssm_conv_w, l1_ssm_conv_b, l1_ssm_dt_bias, l1_ssm_a_log, l1_ssm_d_skip, l1_ssm_norm_g, l1_ssm_w_out, l2_nsa_w_in, l2_nsa_cmp_pos_k, l2_nsa_cmp_pos_v, l2_nsa_cmp_k_w1, l2_nsa_cmp_k_w2, l2_nsa_cmp_v_w1, l2_nsa_cmp_v_w2, l2_nsa_w_out, l3_sg_w_in, l3_sg_vnorm_g, l3_sg_vnorm_b, l3_sg_w_s, l3_sg_b_s, l3_sg_w_out):
    batch, seq, d = x.shape
    x2 = x.reshape(batch * seq, d)
    w1 = ff_w1.astype(BF16)
    w2 = ff_w2.astype(BF16)

    x2 = _gmlp_layer(x2, norm_gains[0, 0], l0_sg_w_in, l0_sg_vnorm_g, l0_sg_vnorm_b, l0_sg_w_s, l0_sg_b_s, l0_sg_w_out)
    x2 = _mlp(x2, norm_gains[0, 1], w1[0], w2[0])
    x2 = _mamba_layer(x2, norm_gains[1, 0], l1_ssm_w_in, l1_ssm_conv_w, l1_ssm_conv_b, l1_ssm_dt_bias,
                      l1_ssm_a_log, l1_ssm_d_skip, l1_ssm_norm_g, l1_ssm_w_out, batch, seq)
    x2 = _mlp(x2, norm_gains[1, 1], w1[1], w2[1])
    x2 = _nsa_layer(x2, norm_gains[2, 0], l2_nsa_w_in, l2_nsa_cmp_pos_k, l2_nsa_cmp_pos_v, l2_nsa_cmp_k_w1,
                    l2_nsa_cmp_k_w2, l2_nsa_cmp_v_w1, l2_nsa_cmp_v_w2, l2_nsa_w_out, batch, seq)
    x2 = _mlp(x2, norm_gains[2, 1], w1[2], w2[2])
    x2 = _gmlp_layer(x2, norm_gains[3, 0], l3_sg_w_in, l3_sg_vnorm_g, l3_sg_vnorm_b, l3_sg_w_s, l3_sg_b_s, l3_sg_w_out)
    x2 = _mlp(x2, norm_gains[3, 1], w1[3], w2[3], final_gain=final_norm)
    return x2.reshape(batch, seq, d)
```

```python
import functools

import numpy as np
import jax
import jax.numpy as jnp
from jax import lax
from jax.experimental import pallas as pl
from jax.experimental.pallas import tpu as pltpu

F32 = jnp.float32
BF16 = jnp.bfloat16

D_MODEL = 1024
D_FF = 4 * D_MODEL
NORM_EPS = 1e-5
NEG_INF = -1e30
FORCE = 1e6
SQRT_HALF = 0.7071067811865476
LOG2E = 1.4426950408889634

SG_CHUNK = 128
SG_WIDTH = 2 * D_MODEL
SG_GROUPS = 8
SG_GW = SG_WIDTH // SG_GROUPS

SSM_D_INNER = 2 * D_MODEL
SSM_HEAD_DIM = 64
SSM_HEADS = SSM_D_INNER // SSM_HEAD_DIM
SSM_GROUPS = 8
SSM_STATE = 128
SSM_CONV = 4
SSM_CHUNK = 128
SSM_GN = SSM_GROUPS * SSM_STATE
SSM_CONV_CH = SSM_D_INNER + 2 * SSM_GN
SSM_GW = SSM_D_INNER // SSM_GROUPS
SSM_HPG = SSM_HEADS // SSM_GROUPS

NSA_HEADS = 16
NSA_KV = 4
NSA_DH = 64
NSA_REP = NSA_HEADS // NSA_KV
CMP_BLOCK = 32
CMP_STRIDE = 16
CMP_HIDDEN = 4 * NSA_DH
SEL_BLOCK = 64
SEL_TOP_N = 16
WINDOW = 512
NSA_TQ = 128
NSA_TK = 256
NSA_WSPAN = WINDOW + NSA_TQ

LANE = 128
VMEM_LIMIT = 56 * 1024 * 1024


def _params(sem):
    return pltpu.CompilerParams(dimension_semantics=sem, vmem_limit_bytes=VMEM_LIMIT)


def _resident(shape):
    nd = len(shape)
    return pl.BlockSpec(shape, lambda *_: (0,) * nd, pipeline_mode=pl.Buffered(1))


def _rms(x, g):
    return x * lax.rsqrt(jnp.mean(x * x, axis=-1, keepdims=True) + NORM_EPS) * g


def _gelu(a):
    return 0.5 * a * (1.0 + lax.erf(a * SQRT_HALF))


def _silu(a):
    return a * jax.nn.sigmoid(a)


def _dot(a, b):
    return jnp.dot(a, b, preferred_element_type=F32)


def _dot_nt(a, b):
    return lax.dot_general(a, b, (((1,), (1,)), ((), ())), preferred_element_type=F32)


def _dot_tn(a, b):
    return lax.dot_general(a, b, (((0,), (0,)), ((), ())), preferred_element_type=F32)


def _split3(v):
    hi = v.astype(BF16)
    r = v - hi.astype(F32)
    mid = r.astype(BF16)
    lo = (r - mid.astype(F32)).astype(BF16)
    return hi, mid, lo


def _norm_proj_kernel(x_ref, g_ref, w_ref, *out_refs, outs, tn):
    h = _rms(x_ref[...], g_ref[...]).astype(BF16)
    c0 = 0
    for o_ref, (width, _, scale) in zip(out_refs, outs):
        for c in range(0, width, tn):
            ce = min(c + tn, width)
            a = _dot(h, w_ref[:, c0 + c:c0 + ce])
            if scale != 1.0:
                a = a * scale
            o_ref[:, c:ce] = a.astype(o_ref.dtype)
        c0 += width


def _norm_proj(x2, gain, w, outs, tm=512, tn=512):
    n, d = x2.shape
    wtot = sum(o[0] for o in outs)
    assert w.shape == (d, wtot)
    return pl.pallas_call(
        functools.partial(_norm_proj_kernel, outs=tuple(outs), tn=tn),
        grid=(n // tm,),
        in_specs=[pl.BlockSpec((tm, d), lambda i: (i, 0)),
                  _resident((1, d)),
                  _resident((d, wtot))],
        out_specs=[pl.BlockSpec((tm, o[0]), lambda i: (i, 0)) for o in outs],
        out_shape=[jax.ShapeDtypeStruct((n, o[0]), o[1]) for o in outs],
        compiler_params=_params(("parallel",)),
        name="norm_proj",
    )(x2, gain.reshape(1, d), w)


def _mlp_kernel(*refs, tf, final, mixer_proj):
    if mixer_proj:
        a_ref, wo_ref, x_ref, g_ref, w1_ref, w2_ref, fg_ref, o_ref, h_ref, acc_ref = refs
        x = x_ref[...] + _dot(a_ref[...], wo_ref[...])
    else:
        x_ref, g_ref, w1_ref, w2_ref, fg_ref, o_ref, h_ref, acc_ref = refs
        x = x_ref[...]
    h_ref[...] = _rms(x, g_ref[...]).astype(BF16)
    acc_ref[...] = x
    for c in range(0, D_FF, tf):
        a = _dot(h_ref[...], w1_ref[:, c:c + tf])
        a = jnp.square(jnp.maximum(a, 0.0)).astype(BF16)
        acc_ref[...] += _dot(a, w2_ref[c:c + tf, :])
    y = acc_ref[...]
    if final:
        y = _rms(y, fg_ref[...])
    o_ref[...] = y


def _mlp(x2, gain, w1, w2, final_gain=None, mixer_proj=None, tm=512, tf=512):
    n, d = x2.shape
    final = final_gain is not None
    fg = (final_gain if final else gain).reshape(1, d)
    in_specs = [pl.BlockSpec((tm, d), lambda i: (i, 0)),
                _resident((1, d)),
                _resident((d, D_FF)),
                _resident((D_FF, d)),
                _resident((1, d))]
    args = (x2, gain.reshape(1, d), w1, w2, fg)
    if mixer_proj is not None:
        a, w_out = mixer_proj
        k = a.shape[1]
        in_specs = [pl.BlockSpec((tm, k), lambda i: (i, 0)), _resident((k, d))] + in_specs
        args = (a, w_out) + args
    return pl.pallas_call(
        functools.partial(_mlp_kernel, tf=tf, final=final, mixer_proj=mixer_proj is not None),
        grid=(n // tm,),
        in_specs=in_specs,
        out_specs=pl.BlockSpec((tm, d), lambda i: (i, 0)),
        out_shape=jax.ShapeDtypeStruct((n, d), F32),
        scratch_shapes=[pltpu.VMEM((tm, d), BF16), pltpu.VMEM((tm, d), F32)],
        compiler_params=_params(("parallel",)),
        name="sqrelu_mlp",
    )(*args)


def _gmlp_kernel(x_ref, g_ref, win_ref, lng_ref, lnb_ref, ws_ref, bs_ref, wout_ref, o_ref,
                 h_ref, u_ref, v_ref, vb_ref, gt_ref, *, tm, tn):
    x = x_ref[...]
    h_ref[...] = _rms(x, g_ref[...]).astype(BF16)
    for c in range(0, 2 * SG_WIDTH, tn):
        a = _gelu(_dot(h_ref[...], win_ref[:, c:c + tn]))
        if c < SG_WIDTH:
            u_ref[:, c:c + tn] = a.astype(BF16)
        else:
            v_ref[:, c - SG_WIDTH:c - SG_WIDTH + tn] = a
    v = v_ref[...]
    mu = jnp.mean(v, axis=-1, keepdims=True)
    dv = v - mu
    var = jnp.mean(dv * dv, axis=-1, keepdims=True)
    vb_ref[...] = (dv * lax.rsqrt(var + NORM_EPS) * lng_ref[...] + lnb_ref[...]).astype(BF16)
    row = lax.broadcasted_iota(jnp.int32, (SG_CHUNK, SG_CHUNK), 0)
    col = lax.broadcasted_iota(jnp.int32, (SG_CHUNK, SG_CHUNK), 1)
    causal = col <= row
    for g in range(SG_GROUPS):
        wg = jnp.where(causal, ws_ref[g], 0.0).astype(BF16)
        bg = bs_ref[:, g:g + 1]
        cs = slice(g * SG_GW, (g + 1) * SG_GW)
        for r0 in range(0, tm, SG_CHUNK):
            rs = slice(r0, r0 + SG_CHUNK)
            sv = _dot(wg, vb_ref[rs, cs]) + bg
            gt_ref[rs, cs] = (u_ref[rs, cs].astype(F32) * sv).astype(BF16)
    o_ref[...] = x + _dot(gt_ref[...], wout_ref[...])


def _gmlp_layer(x2, gain, w_in, vnorm_g, vnorm_b, w_s, b_s, w_out, tm=512, tn=512):
    n, d = x2.shape
    return pl.pallas_call(
        functools.partial(_gmlp_kernel, tm=tm, tn=tn),
        grid=(n // tm,),
        in_specs=[pl.BlockSpec((tm, d), lambda i: (i, 0)),
                  _resident((1, d)),
                  _resident((d, 2 * SG_WIDTH)),
                  _resident((1, SG_WIDTH)),
                  _resident((1, SG_WIDTH)),
                  _resident((SG_GROUPS, SG_CHUNK, SG_CHUNK)),
                  _resident((SG_CHUNK, SG_GROUPS)),
                  _resident((SG_WIDTH, d))],
        out_specs=pl.BlockSpec((tm, d), lambda i: (i, 0)),
        out_shape=jax.ShapeDtypeStruct((n, d), F32),
        scratch_shapes=[pltpu.VMEM((tm, d), BF16),
                        pltpu.VMEM((tm, SG_WIDTH), BF16),
                        pltpu.VMEM((tm, SG_WIDTH), F32),
                        pltpu.VMEM((tm, SG_WIDTH), BF16),
                        pltpu.VMEM((tm, SG_WIDTH), BF16)],
        compiler_params=_params(("parallel",)),
        name="gmlp_mixer",
    )(x2, gain.reshape(1, d), w_in.astype(BF16), vnorm_g.reshape(1, -1), vnorm_b.reshape(1, -1),
      w_s, b_s.T, w_out.astype(BF16))


def _ssd_kernel(z_ref, xbc_ref, dt_ref, cw_ref, cb_ref, dtb_ref, alog_ref, dsk_ref, ng_ref,
                tril_ref, e_ref, o_ref, xpad_ref, st_ref, xs_ref, b_ref, c_ref, *, cw):
    L = SSM_CHUNK
    hist = 8

    @pl.when(pl.program_id(1) == 0)
    def _():
        xpad_ref[0:hist, :] = jnp.zeros((hist, SSM_CONV_CH), F32)
        st_ref[...] = jnp.zeros_like(st_ref)

    for j in range(0, SSM_CONV_CH, cw):
        js = slice(j, j + cw)
        xpad_ref[hist:hist + L, js] = xbc_ref[:, js].astype(F32)
        acc = cb_ref[:, js] + cw_ref[SSM_CONV - 1:SSM_CONV, js] * xpad_ref[hist:hist + L, js]
        for k in range(1, SSM_CONV):
            acc = acc + cw_ref[SSM_CONV - 1 - k:SSM_CONV - k, js] * xpad_ref[hist - k:hist - k + L, js]
        xpad_ref[0:hist, js] = xpad_ref[L:L + hist, js]
        act = _silu(acc)
        if j < SSM_D_INNER:
            xs_ref[:, js] = act
        elif j < SSM_D_INNER + SSM_GN:
            b_ref[:, j - SSM_D_INNER:j - SSM_D_INNER + cw] = act.astype(BF16)
        else:
            o0 = j - SSM_D_INNER - SSM_GN
            c_ref[:, o0:o0 + cw] = act.astype(BF16)

    dtr = dt_ref[...] + dtb_ref[...]
    dt = jnp.maximum(dtr, 0.0) + jnp.log1p(jnp.exp(-jnp.abs(dtr)))
    da = dt * (-jnp.exp(alog_ref[...]))
    tril = tril_ref[...]
    cs = sum(_dot(tril, p) for p in _split3(da))
    cs_t = cs.T
    dt3 = _split3(dt)
    cs3 = _split3(cs)
    row = lax.broadcasted_iota(jnp.int32, (L, L), 0)
    col = lax.broadcasted_iota(jnp.int32, (L, L), 1)
    causal = col <= row

    for g in range(SSM_GROUPS):
        sl = slice(g * SSM_GW, (g + 1) * SSM_GW)
        eg = e_ref[:, sl]
        dt_e = sum(_dot(p, eg) for p in dt3)
        cs_e = sum(_dot(p, eg) for p in cs3)
        xs_g = xs_ref[:, sl]
        xdt = xs_g * dt_e
        xdt_b = xdt.astype(BF16)
        cs_last = cs_e[L - 1:L, :]
        xw_b = (xdt * jnp.exp(cs_last - cs_e)).astype(BF16)
        bg = b_ref[:, g * SSM_STATE:(g + 1) * SSM_STATE]
        cg = c_ref[:, g * SSM_STATE:(g + 1) * SSM_STATE]
        cb = _dot_nt(cg, bg)
        st_prev = st_ref[g]
        y_off = _dot(cg, st_prev.astype(BF16)) * jnp.exp(cs_e)
        st_ref[g] = st_prev * jnp.exp(cs_last) + _dot_tn(bg, xw_b)
        ys = []
        for r in range(SSM_HPG):
            h = g * SSM_HPG + r
            seg = cs[:, h:h + 1] - cs_t[h:h + 1, :]
            dec = jnp.exp(jnp.where(causal, seg, NEG_INF))
            m = (cb * dec).astype(BF16)
            ys.append(_dot(m, xdt_b[:, r * SSM_HEAD_DIM:(r + 1) * SSM_HEAD_DIM]))
        y = jnp.concatenate(ys, axis=1) + y_off + dsk_ref[:, sl] * xs_g
        yz = y * _silu(z_ref[:, sl].astype(F32))
        ms = jnp.mean(yz * yz, axis=-1, keepdims=True)
        o_ref[:, sl] = (yz * lax.rsqrt(ms + NORM_EPS) * ng_ref[:, sl]).astype(BF16)


def _ssd_core(z, xbc, dtp, conv_w, conv_b, dt_bias, a_log, d_skip, norm_g, batch, seq):
    L = SSM_CHUNK
    nc = seq // L
    tril = jnp.asarray(np.tril(np.ones((L, L), np.float32)), BF16)
    e_np = np.zeros((LANE, SSM_D_INNER), np.float32)
    for h in range(SSM_HEADS):
        e_np[h, h * SSM_HEAD_DIM:(h + 1) * SSM_HEAD_DIM] = 1.0
    expand = jnp.asarray(e_np, BF16)
    pad = LANE - SSM_HEADS
    dtb = jnp.pad(dt_bias.astype(F32), (0, pad)).reshape(1, LANE)
    alog = jnp.pad(a_log.astype(F32), (0, pad)).reshape(1, LANE)
    dsk = jnp.repeat(d_skip.astype(F32), SSM_HEAD_DIM).reshape(1, SSM_D_INNER)
    row = lambda b, c: (b * nc + c, 0)
    return pl.pallas_call(
        functools.partial(_ssd_kernel, cw=512),
        grid=(batch, nc),
        in_specs=[pl.BlockSpec((L, SSM_D_INNER), row),
                  pl.BlockSpec((L, SSM_CONV_CH), row),
                  pl.BlockSpec((L, LANE), row),
                  _resident((SSM_CONV, SSM_CONV_CH)),
                  _resident((1, SSM_CONV_CH)),
                  _resident((1, LANE)),
                  _resident((1, LANE)),
                  _resident((1, SSM_D_INNER)),
                  _resident((1, SSM_D_INNER)),
                  _resident((L, L)),
                  _resident((LANE, SSM_D_INNER))],
        out_specs=pl.BlockSpec((L, SSM_D_INNER), row),
        out_shape=jax.ShapeDtypeStruct((batch * seq, SSM_D_INNER), BF16),
        scratch_shapes=[pltpu.VMEM((L + 8, SSM_CONV_CH), F32),
                        pltpu.VMEM((SSM_GROUPS, SSM_STATE, SSM_GW), F32),
                        pltpu.VMEM((L, SSM_D_INNER), F32),
                        pltpu.VMEM((L, SSM_GN), BF16),
                        pltpu.VMEM((L, SSM_GN), BF16)],
        compiler_params=_params(("arbitrary", "arbitrary")),
        name="ssd_core",
    )(z, xbc, dtp, conv_w, conv_b.reshape(1, -1), dtb, alog, dsk, norm_g.reshape(1, -1), tril, expand)


def _mamba_layer(x2, gain, w_in, conv_w, conv_b, dt_bias, a_log, d_skip, norm_g, w_out, batch, seq):
    d = x2.shape[1]
    nzx = SSM_D_INNER + SSM_CONV_CH
    w_dt = jnp.pad(w_in[:, nzx:], ((0, 0), (0, LANE - SSM_HEADS)))
    w_cat = jnp.concatenate([w_in[:, :nzx], w_dt], axis=1).astype(BF16)
    z, xbc, dtp = _norm_proj(x2, gain, w_cat,
                             [(SSM_D_INNER, BF16, 1.0), (SSM_CONV_CH, BF16, 1.0), (LANE, F32, 1.0)])
    yn = _ssd_core(z, xbc, dtp, conv_w, conv_b, dt_bias, a_log, d_skip, norm_g, batch, seq)
    return yn, w_out.astype(BF16)


def _cmp_kernel(x_ref, w1_ref, w2_ref, pos_ref, o_ref):
    half = CMP_STRIDE * NSA_DH
    xh = x_ref[...]
    a = _dot(xh, w1_ref[0:half, :])
    b = _dot(xh, w1_ref[half:2 * half, :])
    posb = _dot(jnp.broadcast_to(pos_ref[...], (8, 2 * half)).astype(BF16), w1_ref[...])[0:1, :]
    nrow = xh.shape[0]
    pre = a + pltpu.roll(b, nrow - 1, 0) + posb
    o_ref[...] = _dot(_gelu(pre).astype(BF16), w2_ref[...]).astype(o_ref.dtype)


def _nsa_compress(kv_half, w1, w2, pos):
    two, batch, g, nh, width = kv_half.shape
    return pl.pallas_call(
        _cmp_kernel,
        grid=(two, batch, g),
        in_specs=[pl.BlockSpec((None, None, None, nh, width), lambda t, b, k: (t, b, k, 0, 0)),
                  pl.BlockSpec((None, width * 2, CMP_HIDDEN), lambda t, b, k: (t, 0, 0)),
                  pl.BlockSpec((None, CMP_HIDDEN, NSA_DH), lambda t, b, k: (t, 0, 0)),
                  pl.BlockSpec((None, 1, width * 2), lambda t, b, k: (t, 0, 0))],
        out_specs=pl.BlockSpec((None, None, None, nh, NSA_DH), lambda t, b, k: (t, b, k, 0, 0)),
        out_shape=jax.ShapeDtypeStruct((two, batch, g, nh, NSA_DH), BF16),
        compiler_params=_params(("parallel", "parallel", "parallel")),
        name="nsa_compress",
    )(kv_half, w1, w2, pos)


def _nsa_attn_kernel(q_ref, gl_ref, kv_ref, kc_ref, vc_ref, ovt_ref, ind_ref,
                     o_ref, ksp_ref, vsp_ref, kws_ref, vwp_ref, vcp_ref, m_ref, acc_ref, sw_ref, pw_ref, part_ref, gs_ref):
    R, TQ, TK, DH, NG = NSA_REP, NSA_TQ, NSA_TK, NSA_DH, NSA_KV
    qi = pl.program_id(1)
    t0 = pl.multiple_of(qi * TQ, TQ)
    seq = kv_ref.shape[0]
    ncp = kc_ref.shape[1]
    nsel = seq // SEL_BLOCK
    groups = range(NG)

    @pl.when(qi == 0)
    def _():
        def with_ones(v):
            one = jnp.where(lax.broadcasted_iota(jnp.int32, (v.shape[0], DH), 1) == 0, 1.0, 0.0)
            return jnp.concatenate([v, one.astype(BF16)], axis=1)

        def part(kind, a):
            c0 = (kind * NG + a) * DH
            return kv_ref[:, c0:c0 + DH]
        for a in groups:
            ksp_ref[a] = jnp.concatenate([part(0, a), ind_ref[...]], axis=1)
            vsp_ref[a] = with_ones(part(1, a))
            kws_ref[a] = part(2, a)
            vwp_ref[a] = with_ones(part(3, a))
            vcp_ref[a] = with_ones(vc_ref[a])

    q_heads = [[q_ref[:, (a * R + r) * DH:(a * R + r + 1) * DH] for r in range(R)] for a in groups]
    qs = [jnp.concatenate(q_heads[a], axis=0) for a in groups]
    ti = t0 + lax.broadcasted_iota(jnp.int32, (TQ, 1), 0)

    def rows(a, r):
        return a[r * TQ:(r + 1) * TQ]

    def rowmax(s):
        return jnp.broadcast_to(jnp.max(s, axis=-1, keepdims=True), (TQ, LANE))

    n_i = lax.broadcasted_iota(jnp.int32, (TQ, ncp), 1)
    valid_c = (n_i * CMP_STRIDE + (CMP_BLOCK - 1)) <= ti
    bias_c = jnp.where(valid_c, 0.0, NEG_INF)
    ovt = ovt_ref[...]
    sc = [_dot_nt(qs[a], kc_ref[a]) for a in groups]
    imp_t = [jnp.zeros((nsel, TQ), F32) for _ in groups]
    e_list = [[] for _ in groups]
    for r in range(R):
        for a in groups:
            s = rows(sc[a], r) + bias_c
            e = jnp.where(valid_c, jnp.exp2(s - rowmax(s)), 0.0)
            parts = _split3(e)
            e_list[a].append(parts[0])
            u = sum(_dot_nt(ovt, p) for p in parts)
            l_t = u[nsel:nsel + 1, :]
            imp_t[a] = imp_t[a] + u[0:nsel, :] / jnp.where(l_t > 0.0, l_t, 1.0)
    oc = [_dot(jnp.concatenate(e_list[a], axis=0), vcp_ref[a]) for a in groups]

    j_i = lax.broadcasted_iota(jnp.int32, (nsel, TQ), 0)
    t_l = t0 + lax.broadcasted_iota(jnp.int32, (nsel, TQ), 1)
    cur = lax.shift_right_logical(t_l, 6)
    forced = (j_i == 0) | (j_i == cur) | (j_i == cur - 1)
    future = j_i * SEL_BLOCK > t_l
    val = [jnp.where(forced, FORCE, jnp.where(future, -FORCE, imp_t[a])) for a in groups]
    cnt = [jnp.zeros((nsel, TQ), F32) for _ in groups]
    for i in range(nsel):
        for a in groups:
            vi = val[a][i:i + 1, :]
            ge = jnp.where(vi >= val[a], 1.0, 0.0)
            gt = jnp.where(vi > val[a], 1.0, 0.0)
            cnt[a] = cnt[a] + jnp.where(j_i > i, ge, gt)
    in_loop = j_i < qi * (TQ // SEL_BLOCK)
    qsp = []
    for a in groups:
        sb_t = jnp.where((cnt[a] < float(SEL_TOP_N)) & in_loop, 0.0, NEG_INF)
        sb = jnp.concatenate([sb_t, jnp.zeros((LANE - nsel, TQ), F32)], axis=0).T
        sbb = sb[:, :DH].astype(BF16)
        qsp.append(jnp.concatenate([jnp.concatenate([qh, sbb], axis=1) for qh in q_heads[a]], axis=0))

    row = lax.broadcasted_iota(jnp.int32, (TQ, TQ), 0)
    col = lax.broadcasted_iota(jnp.int32, (TQ, TQ), 1)
    causal_b = jnp.where(col <= row, 0.0, NEG_INF)
    sd = [_dot_nt(qs[a], ksp_ref[a, pl.ds(t0, TQ), 0:DH]) for a in groups]
    for a in groups:
        p_list = []
        for r in range(R):
            s = rows(sd[a], r) + causal_b
            m = rowmax(s)
            m_ref[a, r] = m
            p_list.append(jnp.exp2(s - m).astype(BF16))
        pv = _dot(jnp.concatenate(p_list, axis=0), vsp_ref[a, pl.ds(t0, TQ), :])
        for r in range(R):
            acc_ref[a, r] = rows(pv, r)

    k0w = pl.multiple_of(jnp.maximum(t0 - WINDOW, 0), TQ)
    kp = k0w + lax.broadcasted_iota(jnp.int32, (TQ, NSA_WSPAN), 1)
    bias_w = jnp.where((kp <= ti) & (kp > ti - WINDOW), 0.0, NEG_INF)
    lane_tiles = [slice(j, j + LANE) for j in range(0, NSA_WSPAN, LANE)]
    for a in groups:
        sw_ref[a] = _dot_nt(qs[a], kws_ref[a, pl.ds(k0w, NSA_WSPAN), :])
    for r in range(R):
        for a in groups:
            rs = slice(r * TQ, (r + 1) * TQ)
            m = sw_ref[a, rs, lane_tiles[0]] + bias_w[:, lane_tiles[0]]
            for lt in lane_tiles[1:]:
                m = jnp.maximum(m, sw_ref[a, rs, lt] + bias_w[:, lt])
            m = rowmax(m)
            for lt in lane_tiles:
                pw_ref[a, rs, lt] = jnp.exp2(sw_ref[a, rs, lt] + bias_w[:, lt] - m).astype(BF16)
    pw = [_dot(pw_ref[a], vwp_ref[a, pl.ds(k0w, NSA_WSPAN), :]) for a in groups]

    gates = [jax.nn.sigmoid(gl_ref[:, a * LANE:(a + 1) * LANE]) for a in groups]
    for a in groups:
        for r in range(R):
            a_c, a_w = rows(oc[a], r), rows(pw[a], r)
            l_c = a_c[:, DH:DH + 1]
            w_c = gates[a][:, 3 * r:3 * r + 1] / jnp.where(l_c > 0.0, l_c, 1.0)
            w_w = gates[a][:, 3 * r + 2:3 * r + 3] / a_w[:, DH:DH + 1]
            part_ref[a, r] = w_c * a_c[:, :DH] + w_w * a_w[:, :DH]
            gs_ref[a, r] = jnp.broadcast_to(gates[a][:, 3 * r + 1:3 * r + 2], (TQ, DH))

    def sel_tile(kt, carry):
        k0 = pl.multiple_of(kt * TK, TK)
        s_all = [_dot_nt(qsp[a], ksp_ref[a, pl.ds(k0, TK), :]) for a in groups]
        for a in groups:
            p_list, alphas = [], []
            for r in range(R):
                s = rows(s_all[a], r)
                m_old = m_ref[a, r]
                m_new = jnp.maximum(m_old, rowmax(s))
                m_ref[a, r] = m_new
                p_list.append(jnp.exp2(s - jnp.concatenate([m_new] * (TK // LANE), axis=1)).astype(BF16))
                alphas.append(jnp.exp2(m_old - m_new))
            pv = _dot(jnp.concatenate(p_list, axis=0), vsp_ref[a, pl.ds(k0, TK), :])
            for r in range(R):
                acc_ref[a, r] = alphas[r] * acc_ref[a, r] + rows(pv, r)
        return carry

    lax.fori_loop(0, lax.shift_right_logical(qi + 1, 1), sel_tile, 0)

    outs = []
    for a in groups:
        for r in range(R):
            a_s = acc_ref[a, r]
            outs.append(part_ref[a, r] + (gs_ref[a, r] * a_s[:, :DH]) / a_s[:, DH:DH + 1])
    o_ref[...] = jnp.concatenate(outs, axis=1).astype(o_ref.dtype)


def _nsa_attention(q, gl, kv4, cmp, batch, seq):
    R, TQ, DH, G = NSA_REP, NSA_TQ, NSA_DH, NSA_KV
    nq = seq // TQ
    ncp = cmp.shape[3]
    nsel = seq // SEL_BLOCK
    nc = (seq - CMP_BLOCK) // CMP_STRIDE + 1
    assert ncp == LANE and nsel < DH and NSA_TK % LANE == 0 and TQ % SEL_BLOCK == 0
    ov = np.zeros((DH, ncp), np.float32)
    for n in range(nc):
        for j in range(nsel):
            if n * CMP_STRIDE <= j * SEL_BLOCK + SEL_BLOCK - 1 and n * CMP_STRIDE + CMP_BLOCK - 1 >= j * SEL_BLOCK:
                ov[j, n] = 1.0
    ov[nsel, :] = 1.0
    ind = np.zeros((seq, DH), np.float32)
    for j in range(nsel):
        ind[j * SEL_BLOCK:(j + 1) * SEL_BLOCK, j] = 1.0
    cmp_spec = lambda idx: pl.BlockSpec((None, None, G, ncp, DH), lambda b, i: (idx, b, 0, 0, 0))
    return pl.pallas_call(
        _nsa_attn_kernel,
        grid=(batch, nq),
        in_specs=[pl.BlockSpec((TQ, G * R * DH), lambda b, i: (b * nq + i, 0)),
                  pl.BlockSpec((TQ, G * LANE), lambda b, i: (b * nq + i, 0)),
                  pl.BlockSpec((seq, 4 * G * DH), lambda b, i: (b, 0)),
                  cmp_spec(0), cmp_spec(1),
                  _resident((DH, ncp)),
                  _resident((seq, DH))],
        out_specs=pl.BlockSpec((TQ, G * R * DH), lambda b, i: (b * nq + i, 0)),
        out_shape=jax.ShapeDtypeStruct((batch * seq, NSA_HEADS * DH), BF16),
        scratch_shapes=[pltpu.VMEM((G, seq, 2 * DH), BF16),
                        pltpu.VMEM((G, seq, 2 * DH), BF16),
                        pltpu.VMEM((G, seq, DH), BF16),
                        pltpu.VMEM((G, seq, 2 * DH), BF16),
                        pltpu.VMEM((G, ncp, 2 * DH), BF16),
                        pltpu.VMEM((G, R, TQ, LANE), F32),
                        pltpu.VMEM((G, R, TQ, LANE), F32),
                        pltpu.VMEM((G, R * TQ, NSA_WSPAN), F32),
                        pltpu.VMEM((G, R * TQ, NSA_WSPAN), BF16),
                        pltpu.VMEM((G, R, TQ, DH), F32),
                        pltpu.VMEM((G, R, TQ, DH), F32)],
        compiler_params=_params(("arbitrary", "arbitrary")),
        name="nsa_attention",
    )(q, gl, kv4, cmp, cmp, jnp.asarray(ov, BF16), jnp.asarray(ind, BF16))


def _nsa_layer(x2, gain, w_in, pos_k, pos_v, k_w1, k_w2, v_w1, v_w2, w_out, batch, seq):
    d = x2.shape[1]
    nq = NSA_HEADS * NSA_DH
    nkv = 6 * NSA_KV * NSA_DH
    ngate = 3 * NSA_REP
    w_gate = w_in[:, nq + nkv:].reshape(d, NSA_KV, ngate)
    w_gate = jnp.pad(w_gate, ((0, 0), (0, 0), (0, LANE - ngate))).reshape(d, NSA_KV * LANE)
    w_cat = jnp.concatenate([w_in[:, :nq + nkv], w_gate], axis=1).astype(BF16)
    ncv = 2 * NSA_KV * NSA_DH
    q, kcv, kv4, gl = _norm_proj(x2, gain, w_cat,
                                 [(nq, BF16, NSA_DH ** -0.5 * LOG2E), (ncv, BF16, 1.0), (nkv - ncv, BF16, 1.0),
                                  (NSA_KV * LANE, F32, 1.0)])
    kcv_t = kcv.reshape(batch, seq, 2, NSA_KV, NSA_DH).transpose(2, 0, 3, 1, 4)
    half = kcv_t.reshape(2, batch, NSA_KV, seq // CMP_STRIDE, CMP_STRIDE * NSA_DH)
    cmp = _nsa_compress(half,
                        jnp.stack([k_w1, v_w1]).astype(BF16),
                        jnp.stack([k_w2, v_w2]).astype(BF16),
                        jnp.stack([pos_k.reshape(1, -1), pos_v.reshape(1, -1)]))
    o = _nsa_attention(q, gl, kv4, cmp, batch, seq)
    return o, w_out.astype(BF16)


def kernel(x, norm_gains, final_norm, ff_w1, ff_w2, l0_sg_w_in, l0_sg_vnorm_g, l0_sg_vnorm_b, l0_sg_w_s, l0_sg_b_s, l0_sg_w_out, l1_ssm_w_in, l1_ssm_conv_w, l1_ssm_conv_b, l1_ssm_dt_bias, l1_ssm_a_log, l1_ssm_d_skip, l1_ssm_norm_g, l1_ssm_w_out, l2_nsa_w_in, l2_nsa_cmp_pos_k, l2_nsa_cmp_pos_v, l2_nsa_cmp_k_w1, l2_nsa_cmp_k_w2, l2_nsa_cmp_v_w1, l2_nsa_cmp_v_w2, l2_nsa_w_out, l3_sg_w_in, l3_sg_vnorm_g, l3_sg_vnorm_b, l3_sg_w_s, l3_sg_b_s, l3_sg_w_out):
    batch, seq, d = x.shape
    x2 = x.reshape(batch * seq, d)
    w1 = ff_w1.astype(BF16)
    w2 = ff_w2.astype(BF16)

    x2 = _gmlp_layer(x2, norm_gains[0, 0], l0_sg_w_in, l0_sg_vnorm_g, l0_sg_vnorm_b, l0_sg_w_s, l0_sg_b_s, l0_sg_w_out)
    x2 = _mlp(x2, norm_gains[0, 1], w1[0], w2[0])
    mix = _mamba_layer(x2, norm_gains[1, 0], l1_ssm_w_in, l1_ssm_conv_w, l1_ssm_conv_b, l1_ssm_dt_bias,
                       l1_ssm_a_log, l1_ssm_d_skip, l1_ssm_norm_g, l1_ssm_w_out, batch, seq)
    x2 = _mlp(x2, norm_gains[1, 1], w1[1], w2[1], mixer_proj=mix)
    mix = _nsa_layer(x2, norm_gains[2, 0], l2_nsa_w_in, l2_nsa_cmp_pos_k, l2_nsa_cmp_pos_v, l2_nsa_cmp_k_w1,
                     l2_nsa_cmp_k_w2, l2_nsa_cmp_v_w1, l2_nsa_cmp_v_w2, l2_nsa_w_out, batch, seq)
    x2 = _mlp(x2, norm_gains[2, 1], w1[2], w2[2], mixer_proj=mix)
    x2 = _gmlp_layer(x2, norm_gains[3, 0], l3_sg_w_in, l3_sg_vnorm_g, l3_sg_vnorm_b, l3_sg_w_s, l3_sg_b_s, l3_sg_w_out)
    x2 = _mlp(x2, norm_gains[3, 1], w1[3], w2[3], final_gain=final_norm)
    return x2.reshape(batch, seq, d)
```

```python
import functools

import numpy as np
import jax
import jax.numpy as jnp
from jax import lax
from jax.experimental import pallas as pl
from jax.experimental.pallas import tpu as pltpu

F32 = jnp.float32
BF16 = jnp.bfloat16

D_MODEL = 1024
D_FF = 4 * D_MODEL
NORM_EPS = 1e-5
NEG_INF = -1e30
FORCE = 1e6
SQRT_HALF = 0.7071067811865476
LOG2E = 1.4426950408889634

SG_CHUNK = 128
SG_WIDTH = 2 * D_MODEL
SG_GROUPS = 8
SG_GW = SG_WIDTH // SG_GROUPS

SSM_D_INNER = 2 * D_MODEL
SSM_HEAD_DIM = 64
SSM_HEADS = SSM_D_INNER // SSM_HEAD_DIM
SSM_GROUPS = 8
SSM_STATE = 128
SSM_CONV = 4
SSM_CHUNK = 128
SSM_GN = SSM_GROUPS * SSM_STATE
SSM_CONV_CH = SSM_D_INNER + 2 * SSM_GN
SSM_GW = SSM_D_INNER // SSM_GROUPS
SSM_HPG = SSM_HEADS // SSM_GROUPS

NSA_HEADS = 16
NSA_KV = 4
NSA_DH = 64
NSA_REP = NSA_HEADS // NSA_KV
CMP_BLOCK = 32
CMP_STRIDE = 16
CMP_HIDDEN = 4 * NSA_DH
SEL_BLOCK = 64
SEL_TOP_N = 16
WINDOW = 512
NSA_TQ = 128
NSA_TK = 256
NSA_WSPAN = WINDOW + NSA_TQ
NSA_VROWS = NSA_DH + 16

LANE = 128
VMEM_LIMIT = 56 * 1024 * 1024


def _params(sem):
    return pltpu.CompilerParams(dimension_semantics=sem, vmem_limit_bytes=VMEM_LIMIT)


def _resident(shape):
    nd = len(shape)
    return pl.BlockSpec(shape, lambda *_: (0,) * nd, pipeline_mode=pl.Buffered(1))


def _rms(x, g):
    return x * lax.rsqrt(jnp.mean(x * x, axis=-1, keepdims=True) + NORM_EPS) * g


def _gelu(a):
    return 0.5 * a * (1.0 + lax.erf(a * SQRT_HALF))


def _silu(a):
    return a * jax.nn.sigmoid(a)


def _dot(a, b):
    return jnp.dot(a, b, preferred_element_type=F32)


def _dot_nt(a, b):
    return lax.dot_general(a, b, (((1,), (1,)), ((), ())), preferred_element_type=F32)


def _dot_tn(a, b):
    return lax.dot_general(a, b, (((0,), (0,)), ((), ())), preferred_element_type=F32)


def _split3(v):
    hi = v.astype(BF16)
    r = v - hi.astype(F32)
    mid = r.astype(BF16)
    lo = (r - mid.astype(F32)).astype(BF16)
    return hi, mid, lo


def _norm_proj_kernel(x_ref, g_ref, w_ref, *out_refs, outs, tn):
    h = _rms(x_ref[...], g_ref[...]).astype(BF16)
    c0 = 0
    for o_ref, (width, _, scale) in zip(out_refs, outs):
        for c in range(0, width, tn):
            ce = min(c + tn, width)
            a = _dot(h, w_ref[:, c0 + c:c0 + ce])
            if scale != 1.0:
                a = a * scale
            o_ref[:, c:ce] = a.astype(o_ref.dtype)
        c0 += width


def _norm_proj(x2, gain, w, outs, tm=512, tn=512):
    n, d = x2.shape
    wtot = sum(o[0] for o in outs)
    assert w.shape == (d, wtot)
    return pl.pallas_call(
        functools.partial(_norm_proj_kernel, outs=tuple(outs), tn=tn),
        grid=(n // tm,),
        in_specs=[pl.BlockSpec((tm, d), lambda i: (i, 0)),
                  _resident((1, d)),
                  _resident((d, wtot))],
        out_specs=[pl.BlockSpec((tm, o[0]), lambda i: (i, 0)) for o in outs],
        out_shape=[jax.ShapeDtypeStruct((n, o[0]), o[1]) for o in outs],
        compiler_params=_params(("parallel",)),
        name="norm_proj",
    )(x2, gain.reshape(1, d), w)


def _mlp_kernel(*refs, tf, final, mixer_proj):
    if mixer_proj:
        a_ref, wo_ref, x_ref, g_ref, w1_ref, w2_ref, fg_ref, o_ref, h_ref, acc_ref = refs
        x = x_ref[...] + _dot(a_ref[...], wo_ref[...])
    else:
        x_ref, g_ref, w1_ref, w2_ref, fg_ref, o_ref, h_ref, acc_ref = refs
        x = x_ref[...]
    h_ref[...] = _rms(x, g_ref[...]).astype(BF16)
    acc_ref[...] = x
    for c in range(0, D_FF, tf):
        a = _dot(h_ref[...], w1_ref[:, c:c + tf])
        a = jnp.square(jnp.maximum(a, 0.0)).astype(BF16)
        acc_ref[...] += _dot(a, w2_ref[c:c + tf, :])
    y = acc_ref[...]
    if final:
        y = _rms(y, fg_ref[...])
    o_ref[...] = y


def _mlp(x2, gain, w1, w2, final_gain=None, mixer_proj=None, tm=512, tf=512):
    n, d = x2.shape
    final = final_gain is not None
    fg = (final_gain if final else gain).reshape(1, d)
    in_specs = [pl.BlockSpec((tm, d), lambda i: (i, 0)),
                _resident((1, d)),
                _resident((d, D_FF)),
                _resident((D_FF, d)),
                _resident((1, d))]
    args = (x2, gain.reshape(1, d), w1, w2, fg)
    if mixer_proj is not None:
        a, w_out = mixer_proj
        k = a.shape[1]
        in_specs = [pl.BlockSpec((tm, k), lambda i: (i, 0)), _resident((k, d))] + in_specs
        args = (a, w_out) + args
    return pl.pallas_call(
        functools.partial(_mlp_kernel, tf=tf, final=final, mixer_proj=mixer_proj is not None),
        grid=(n // tm,),
        in_specs=in_specs,
        out_specs=pl.BlockSpec((tm, d), lambda i: (i, 0)),
        out_shape=jax.ShapeDtypeStruct((n, d), F32),
        scratch_shapes=[pltpu.VMEM((tm, d), BF16), pltpu.VMEM((tm, d), F32)],
        compiler_params=_params(("parallel",)),
        name="sqrelu_mlp",
    )(*args)


def _gmlp_kernel(x_ref, g_ref, win_ref, lng_ref, lnb_ref, ws_ref, bs_ref, wout_ref, o_ref,
                 h_ref, u_ref, v_ref, vb_ref, gt_ref, *, tm, tn):
    x = x_ref[...]
    h_ref[...] = _rms(x, g_ref[...]).astype(BF16)
    for c in range(0, 2 * SG_WIDTH, tn):
        a = _gelu(_dot(h_ref[...], win_ref[:, c:c + tn]))
        if c < SG_WIDTH:
            u_ref[:, c:c + tn] = a.astype(BF16)
        else:
            v_ref[:, c - SG_WIDTH:c - SG_WIDTH + tn] = a
    v = v_ref[...]
    mu = jnp.mean(v, axis=-1, keepdims=True)
    dv = v - mu
    var = jnp.mean(dv * dv, axis=-1, keepdims=True)
    vb_ref[...] = (dv * lax.rsqrt(var + NORM_EPS) * lng_ref[...] + lnb_ref[...]).astype(BF16)
    row = lax.broadcasted_iota(jnp.int32, (SG_CHUNK, SG_CHUNK), 0)
    col = lax.broadcasted_iota(jnp.int32, (SG_CHUNK, SG_CHUNK), 1)
    causal = col <= row
    for g in range(SG_GROUPS):
        wg = jnp.where(causal, ws_ref[g], 0.0).astype(BF16)
        bg = bs_ref[:, g:g + 1]
        cs = slice(g * SG_GW, (g + 1) * SG_GW)
        for r0 in range(0, tm, SG_CHUNK):
            rs = slice(r0, r0 + SG_CHUNK)
            sv = _dot(wg, vb_ref[rs, cs]) + bg
            gt_ref[rs, cs] = (u_ref[rs, cs].astype(F32) * sv).astype(BF16)
    o_ref[...] = x + _dot(gt_ref[...], wout_ref[...])


def _gmlp_layer(x2, gain, w_in, vnorm_g, vnorm_b, w_s, b_s, w_out, tm=512, tn=512):
    n, d = x2.shape
    return pl.pallas_call(
        functools.partial(_gmlp_kernel, tm=tm, tn=tn),
        grid=(n // tm,),
        in_specs=[pl.BlockSpec((tm, d), lambda i: (i, 0)),
                  _resident((1, d)),
                  _resident((d, 2 * SG_WIDTH)),
                  _resident((1, SG_WIDTH)),
                  _resident((1, SG_WIDTH)),
                  _resident((SG_GROUPS, SG_CHUNK, SG_CHUNK)),
                  _resident((SG_CHUNK, SG_GROUPS)),
                  _resident((SG_WIDTH, d))],
        out_specs=pl.BlockSpec((tm, d), lambda i: (i, 0)),
        out_shape=jax.ShapeDtypeStruct((n, d), F32),
        scratch_shapes=[pltpu.VMEM((tm, d), BF16),
                        pltpu.VMEM((tm, SG_WIDTH), BF16),
                        pltpu.VMEM((tm, SG_WIDTH), F32),
                        pltpu.VMEM((tm, SG_WIDTH), BF16),
                        pltpu.VMEM((tm, SG_WIDTH), BF16)],
        compiler_params=_params(("parallel",)),
        name="gmlp_mixer",
    )(x2, gain.reshape(1, d), w_in.astype(BF16), vnorm_g.reshape(1, -1), vnorm_b.reshape(1, -1),
      w_s, b_s.T, w_out.astype(BF16))


def _ssd_kernel(z_ref, xbc_ref, dt_ref, cw_ref, cb_ref, dtb_ref, alog_ref, dsk_ref, ng_ref,
                tril_ref, e_ref, o_ref, xpad_ref, st_ref, xs_ref, b_ref, c_ref, *, cw):
    L = SSM_CHUNK
    hist = 8

    @pl.when(pl.program_id(1) == 0)
    def _():
        xpad_ref[0:hist, :] = jnp.zeros((hist, SSM_CONV_CH), F32)
        st_ref[...] = jnp.zeros_like(st_ref)

    for j in range(0, SSM_CONV_CH, cw):
        js = slice(j, j + cw)
        xpad_ref[hist:hist + L, js] = xbc_ref[:, js].astype(F32)
        acc = cb_ref[:, js] + cw_ref[SSM_CONV - 1:SSM_CONV, js] * xpad_ref[hist:hist + L, js]
        for k in range(1, SSM_CONV):
            acc = acc + cw_ref[SSM_CONV - 1 - k:SSM_CONV - k, js] * xpad_ref[hist - k:hist - k + L, js]
        xpad_ref[0:hist, js] = xpad_ref[L:L + hist, js]
        act = _silu(acc)
        if j < SSM_D_INNER:
            xs_ref[:, js] = act
        elif j < SSM_D_INNER + SSM_GN:
            b_ref[:, j - SSM_D_INNER:j - SSM_D_INNER + cw] = act.astype(BF16)
        else:
            o0 = j - SSM_D_INNER - SSM_GN
            c_ref[:, o0:o0 + cw] = act.astype(BF16)

    dtr = dt_ref[...] + dtb_ref[...]
    dt = jnp.maximum(dtr, 0.0) + jnp.log1p(jnp.exp(-jnp.abs(dtr)))
    da = dt * (-jnp.exp(alog_ref[...]))
    tril = tril_ref[...]
    cs = sum(_dot(tril, p) for p in _split3(da))
    cs_t = cs.T
    dt3 = _split3(dt)
    cs3 = _split3(cs)
    row = lax.broadcasted_iota(jnp.int32, (L, L), 0)
    col = lax.broadcasted_iota(jnp.int32, (L, L), 1)
    causal = col <= row

    for g in range(SSM_GROUPS):
        sl = slice(g * SSM_GW, (g + 1) * SSM_GW)
        eg = e_ref[:, sl]
        dt_e = sum(_dot(p, eg) for p in dt3)
        cs_e = sum(_dot(p, eg) for p in cs3)
        xs_g = xs_ref[:, sl]
        xdt = xs_g * dt_e
        xdt_b = xdt.astype(BF16)
        cs_last = cs_e[L - 1:L, :]
        xw_b = (xdt * jnp.exp(cs_last - cs_e)).astype(BF16)
        bg = b_ref[:, g * SSM_STATE:(g + 1) * SSM_STATE]
        cg = c_ref[:, g * SSM_STATE:(g + 1) * SSM_STATE]
        cb = _dot_nt(cg, bg)
        st_prev = st_ref[g]
        y_off = _dot(cg, st_prev.astype(BF16)) * jnp.exp(cs_e)
        st_ref[g] = st_prev * jnp.exp(cs_last) + _dot_tn(bg, xw_b)
        ys = []
        for r in range(SSM_HPG):
            h = g * SSM_HPG + r
            seg = cs[:, h:h + 1] - cs_t[h:h + 1, :]
            dec = jnp.exp(jnp.where(causal, seg, NEG_INF))
            m = (cb * dec).astype(BF16)
            ys.append(_dot(m, xdt_b[:, r * SSM_HEAD_DIM:(r + 1) * SSM_HEAD_DIM]))
        y = jnp.concatenate(ys, axis=1) + y_off + dsk_ref[:, sl] * xs_g
        yz = y * _silu(z_ref[:, sl].astype(F32))
        ms = jnp.mean(yz * yz, axis=-1, keepdims=True)
        o_ref[:, sl] = (yz * lax.rsqrt(ms + NORM_EPS) * ng_ref[:, sl]).astype(BF16)


def _ssd_core(z, xbc, dtp, conv_w, conv_b, dt_bias, a_log, d_skip, norm_g, batch, seq):
    L = SSM_CHUNK
    nc = seq // L
    tril = jnp.asarray(np.tril(np.ones((L, L), np.float32)), BF16)
    e_np = np.zeros((LANE, SSM_D_INNER), np.float32)
    for h in range(SSM_HEADS):
        e_np[h, h * SSM_HEAD_DIM:(h + 1) * SSM_HEAD_DIM] = 1.0
    expand = jnp.asarray(e_np, BF16)
    pad = LANE - SSM_HEADS
    dtb = jnp.pad(dt_bias.astype(F32), (0, pad)).reshape(1, LANE)
    alog = jnp.pad(a_log.astype(F32), (0, pad)).reshape(1, LANE)
    dsk = jnp.repeat(d_skip.astype(F32), SSM_HEAD_DIM).reshape(1, SSM_D_INNER)
    row = lambda b, c: (b * nc + c, 0)
    return pl.pallas_call(
        functools.partial(_ssd_kernel, cw=512),
        grid=(batch, nc),
        in_specs=[pl.BlockSpec((L, SSM_D_INNER), row),
                  pl.BlockSpec((L, SSM_CONV_CH), row),
                  pl.BlockSpec((L, LANE), row),
                  _resident((SSM_CONV, SSM_CONV_CH)),
                  _resident((1, SSM_CONV_CH)),
                  _resident((1, LANE)),
                  _resident((1, LANE)),
                  _resident((1, SSM_D_INNER)),
                  _resident((1, SSM_D_INNER)),
                  _resident((L, L)),
                  _resident((LANE, SSM_D_INNER))],
        out_specs=pl.BlockSpec((L, SSM_D_INNER), row),
        out_shape=jax.ShapeDtypeStruct((batch * seq, SSM_D_INNER), BF16),
        scratch_shapes=[pltpu.VMEM((L + 8, SSM_CONV_CH), F32),
                        pltpu.VMEM((SSM_GROUPS, SSM_STATE, SSM_GW), F32),
                        pltpu.VMEM((L, SSM_D_INNER), F32),
                        pltpu.VMEM((L, SSM_GN), BF16),
                        pltpu.VMEM((L, SSM_GN), BF16)],
        compiler_params=_params(("arbitrary", "arbitrary")),
        name="ssd_core",
    )(z, xbc, dtp, conv_w, conv_b.reshape(1, -1), dtb, alog, dsk, norm_g.reshape(1, -1), tril, expand)


def _mamba_layer(x2, gain, w_in, conv_w, conv_b, dt_bias, a_log, d_skip, norm_g, w_out, batch, seq):
    d = x2.shape[1]
    nzx = SSM_D_INNER + SSM_CONV_CH
    w_dt = jnp.pad(w_in[:, nzx:], ((0, 0), (0, LANE - SSM_HEADS)))
    w_cat = jnp.concatenate([w_in[:, :nzx], w_dt], axis=1).astype(BF16)
    z, xbc, dtp = _norm_proj(x2, gain, w_cat,
                             [(SSM_D_INNER, BF16, 1.0), (SSM_CONV_CH, BF16, 1.0), (LANE, F32, 1.0)])
    yn = _ssd_core(z, xbc, dtp, conv_w, conv_b, dt_bias, a_log, d_skip, norm_g, batch, seq)
    return yn, w_out.astype(BF16)


def _cmp_kernel(x_ref, w1_ref, w2_ref, pos_ref, o_ref):
    half = CMP_STRIDE * NSA_DH
    xh = x_ref[...]
    a = _dot(xh, w1_ref[0:half, :])
    b = _dot(xh, w1_ref[half:2 * half, :])
    posb = _dot(jnp.broadcast_to(pos_ref[...], (8, 2 * half)).astype(BF16), w1_ref[...])[0:1, :]
    nrow = xh.shape[0]
    pre = a + pltpu.roll(b, nrow - 1, 0) + posb
    o_ref[...] = _dot(_gelu(pre).astype(BF16), w2_ref[...]).astype(o_ref.dtype)


def _nsa_compress(kv_half, w1, w2, pos):
    two, batch, g, nh, width = kv_half.shape
    return pl.pallas_call(
        _cmp_kernel,
        grid=(two, batch, g),
        in_specs=[pl.BlockSpec((None, None, None, nh, width), lambda t, b, k: (t, b, k, 0, 0)),
                  pl.BlockSpec((None, width * 2, CMP_HIDDEN), lambda t, b, k: (t, 0, 0)),
                  pl.BlockSpec((None, CMP_HIDDEN, NSA_DH), lambda t, b, k: (t, 0, 0)),
                  pl.BlockSpec((None, 1, width * 2), lambda t, b, k: (t, 0, 0))],
        out_specs=pl.BlockSpec((None, None, None, nh, NSA_DH), lambda t, b, k: (t, b, k, 0, 0)),
        out_shape=jax.ShapeDtypeStruct((two, batch, g, nh, NSA_DH), BF16),
        compiler_params=_params(("parallel", "parallel", "parallel")),
        name="nsa_compress",
    )(kv_half, w1, w2, pos)


def _nsa_attn_kernel(q_ref, gl_ref, kv_ref, kc_ref, vc_ref, ovt_ref, ind_ref,
                     o_ref, ksp_ref, kws_ref, vst_ref, vwt_ref, vct_ref, m_ref, acc_ref):
    R, TQ, TK, DH, NG = NSA_REP, NSA_TQ, NSA_TK, NSA_DH, NSA_KV
    W = R * TQ
    qi = pl.program_id(1)
    t0 = pl.multiple_of(qi * TQ, TQ)
    seq = kv_ref.shape[0]
    ncp = kc_ref.shape[1]
    nsel = seq // SEL_BLOCK
    groups = range(NG)

    @pl.when(qi == 0)
    def _():
        def part(kind, a):
            c0 = (kind * NG + a) * DH
            return kv_ref[:, c0:c0 + DH]

        def t_ones(v):
            n = v.shape[0]
            one = jnp.where(lax.broadcasted_iota(jnp.int32, (n, LANE - DH), 1) == 0, 1.0, 0.0)
            vp = jnp.concatenate([v.astype(F32), one], axis=1)
            vt = jnp.concatenate([vp[j:j + LANE].T for j in range(0, n, LANE)], axis=1)
            return vt[0:NSA_VROWS].astype(BF16)
        for a in groups:
            ksp_ref[a] = jnp.concatenate([part(0, a), ind_ref[...]], axis=1)
            kws_ref[a] = part(2, a)
            vst_ref[a] = t_ones(part(1, a))
            vwt_ref[a] = t_ones(part(3, a))
            vct_ref[a] = t_ones(vc_ref[a])

    q_heads = [[q_ref[:, (a * R + r) * DH:(a * R + r + 1) * DH] for r in range(R)] for a in groups]
    qs = [jnp.concatenate(q_heads[a], axis=0) for a in groups]
    t_lane = t0 + lax.broadcasted_iota(jnp.int32, (1, TQ), 1)

    def heads(x):
        return jnp.concatenate([x] * R, axis=1)

    def colmax(s):
        return jnp.max(s, axis=0, keepdims=True)

    n_sub = lax.broadcasted_iota(jnp.int32, (ncp, TQ), 0)
    valid_c = (n_sub * CMP_STRIDE + (CMP_BLOCK - 1)) <= t_lane
    bias_c = heads(jnp.where(valid_c, 0.0, NEG_INF))
    keep_c = heads(jnp.where(valid_c, 1.0, 0.0))
    ovt = ovt_ref[...]
    sc = [_dot_nt(kc_ref[a], qs[a]) + bias_c for a in groups]
    oc, imp_t = [], []
    for a in groups:
        e = jnp.exp2(sc[a] - colmax(sc[a])) * keep_c
        parts = _split3(e)
        u = sum(_dot(ovt, p) for p in parts)
        l_t = u[nsel:nsel + 1, :]
        un = u[0:nsel, :] / jnp.where(l_t > 0.0, l_t, 1.0)
        imp = un[:, 0:TQ]
        for r in range(1, R):
            imp = imp + un[:, r * TQ:(r + 1) * TQ]
        imp_t.append(imp)
        oc.append(_dot(vct_ref[a], parts[0]))

    j_i = lax.broadcasted_iota(jnp.int32, (nsel, TQ), 0)
    t_l = t0 + lax.broadcasted_iota(jnp.int32, (nsel, TQ), 1)
    cur = lax.shift_right_logical(t_l, 6)
    forced = (j_i == 0) | (j_i == cur) | (j_i == cur - 1)
    future = j_i * SEL_BLOCK > t_l
    val = [jnp.where(forced, FORCE, jnp.where(future, -FORCE, imp_t[a])) for a in groups]
    cnt = [jnp.zeros((nsel, TQ), F32) for _ in groups]
    for i in range(nsel):
        for a in groups:
            vi = val[a][i:i + 1, :]
            ge = jnp.where(vi >= val[a], 1.0, 0.0)
            gt = jnp.where(vi > val[a], 1.0, 0.0)
            cnt[a] = cnt[a] + jnp.where(j_i > i, ge, gt)
    in_loop = j_i < qi * (TQ // SEL_BLOCK)
    qsp = []
    for a in groups:
        sb_t = jnp.where((cnt[a] < float(SEL_TOP_N)) & in_loop, 0.0, NEG_INF)
        sb = jnp.concatenate([sb_t, jnp.zeros((LANE - nsel, TQ), F32)], axis=0).T
        sbb = sb[:, :DH].astype(BF16)
        qsp.append(jnp.concatenate([jnp.concatenate([qh, sbb], axis=1) for qh in q_heads[a]], axis=0))

    k_sub = lax.broadcasted_iota(jnp.int32, (TQ, TQ), 0)
    i_lane = lax.broadcasted_iota(jnp.int32, (TQ, TQ), 1)
    causal_b = heads(jnp.where(k_sub <= i_lane, 0.0, NEG_INF))
    sd = [_dot_nt(ksp_ref[a, pl.ds(t0, TQ), 0:DH], qs[a]) + causal_b for a in groups]
    for a in groups:
        m = colmax(sd[a])
        m_ref[a] = m
        acc_ref[a] = _dot(vst_ref[a, :, pl.ds(t0, TQ)], jnp.exp2(sd[a] - m).astype(BF16))

    k0w = pl.multiple_of(jnp.maximum(t0 - WINDOW, 0), TQ)
    kpos = k0w + lax.broadcasted_iota(jnp.int32, (NSA_WSPAN, TQ), 0)
    bias_w = heads(jnp.where((kpos <= t_lane) & (kpos > t_lane - WINDOW), 0.0, NEG_INF))
    sw = [_dot_nt(kws_ref[a, pl.ds(k0w, NSA_WSPAN), :], qs[a]) + bias_w for a in groups]
    aw = [_dot(vwt_ref[a, :, pl.ds(k0w, NSA_WSPAN)], jnp.exp2(sw[a] - colmax(sw[a])).astype(BF16)) for a in groups]

    gates_t = [jax.nn.sigmoid(gl_ref[:, a * LANE:(a + 1) * LANE]).T for a in groups]

    def gate_row(a, b):
        return jnp.concatenate([gates_t[a][3 * r + b:3 * r + b + 1, :] for r in range(R)], axis=1)

    part = []
    for a in groups:
        l_c = oc[a][DH:DH + 1, :]
        w_c = gate_row(a, 0) / jnp.where(l_c > 0.0, l_c, 1.0)
        w_w = gate_row(a, 2) / aw[a][DH:DH + 1, :]
        part.append(w_c * oc[a][0:DH, :] + w_w * aw[a][0:DH, :])

    def sel_tile(kt, carry):
        k0 = pl.multiple_of(kt * TK, TK)
        s_all = [_dot_nt(ksp_ref[a, pl.ds(k0, TK), :], qsp[a]) for a in groups]
        for a in groups:
            m_old = m_ref[a]
            m_new = jnp.maximum(m_old, colmax(s_all[a]))
            m_ref[a] = m_new
            pv = _dot(vst_ref[a, :, pl.ds(k0, TK)], jnp.exp2(s_all[a] - m_new).astype(BF16))
            acc_ref[a] = jnp.exp2(m_old - m_new) * acc_ref[a] + pv
        return carry

    tq_per_tk = TK // TQ
    lax.fori_loop(0, lax.shift_right_logical(qi + tq_per_tk - 1, tq_per_tk.bit_length() - 1), sel_tile, 0)

    cols = []
    for a in groups:
        acc = acc_ref[a]
        o_t = part[a] + (gate_row(a, 1) / acc[DH:DH + 1, :]) * acc[0:DH, :]
        for r in range(0, R, 2):
            pair = jnp.concatenate([o_t[:, r * TQ:(r + 1) * TQ], o_t[:, (r + 1) * TQ:(r + 2) * TQ]], axis=0)
            cols.append(pair.T)
    o_ref[...] = jnp.concatenate(cols, axis=1).astype(o_ref.dtype)


def _nsa_attention(q, gl, kv4, cmp, batch, seq):
    R, TQ, DH, G = NSA_REP, NSA_TQ, NSA_DH, NSA_KV
    nq = seq // TQ
    ncp = cmp.shape[3]
    nsel = seq // SEL_BLOCK
    nc = (seq - CMP_BLOCK) // CMP_STRIDE + 1
    assert ncp == LANE and nsel < DH and NSA_TK % LANE == 0 and TQ % SEL_BLOCK == 0 and 2 * DH == LANE and R % 2 == 0
    ov = np.zeros((DH, ncp), np.float32)
    for n in range(nc):
        for j in range(nsel):
            if n * CMP_STRIDE <= j * SEL_BLOCK + SEL_BLOCK - 1 and n * CMP_STRIDE + CMP_BLOCK - 1 >= j * SEL_BLOCK:
                ov[j, n] = 1.0
    ov[nsel, :] = 1.0
    ind = np.zeros((seq, DH), np.float32)
    for j in range(nsel):
        ind[j * SEL_BLOCK:(j + 1) * SEL_BLOCK, j] = 1.0
    cmp_spec = lambda idx: pl.BlockSpec((None, None, G, ncp, DH), lambda b, i: (idx, b, 0, 0, 0))
    return pl.pallas_call(
        _nsa_attn_kernel,
        grid=(batch, nq),
        in_specs=[pl.BlockSpec((TQ, G * R * DH), lambda b, i: (b * nq + i, 0)),
                  pl.BlockSpec((TQ, G * LANE), lambda b, i: (b * nq + i, 0)),
                  pl.BlockSpec((seq, 4 * G * DH), lambda b, i: (b, 0)),
                  cmp_spec(0), cmp_spec(1),
                  _resident((DH, ncp)),
                  _resident((seq, DH))],
        out_specs=pl.BlockSpec((TQ, G * R * DH), lambda b, i: (b * nq + i, 0)),
        out_shape=jax.ShapeDtypeStruct((batch * seq, NSA_HEADS * DH), BF16),
        scratch_shapes=[pltpu.VMEM((G, seq, 2 * DH), BF16),
                        pltpu.VMEM((G, seq, DH), BF16),
                        pltpu.VMEM((G, NSA_VROWS, seq), BF16),
                        pltpu.VMEM((G, NSA_VROWS, seq), BF16),
                        pltpu.VMEM((G, NSA_VROWS, ncp), BF16),
                        pltpu.VMEM((G, 1, R * TQ), F32),
                        pltpu.VMEM((G, NSA_VROWS, R * TQ), F32)],
        compiler_params=_params(("arbitrary", "arbitrary")),
        name="nsa_attention",
    )(q, gl, kv4, cmp, cmp, jnp.asarray(ov, BF16), jnp.asarray(ind, BF16))


def _nsa_layer(x2, gain, w_in, pos_k, pos_v, k_w1, k_w2, v_w1, v_w2, w_out, batch, seq):
    d = x2.shape[1]
    nq = NSA_HEADS * NSA_DH
    nkv = 6 * NSA_KV * NSA_DH
    ngate = 3 * NSA_REP
    w_gate = w_in[:, nq + nkv:].reshape(d, NSA_KV, ngate)
    w_gate = jnp.pad(w_gate, ((0, 0), (0, 0), (0, LANE - ngate))).reshape(d, NSA_KV * LANE)
    w_cat = jnp.concatenate([w_in[:, :nq + nkv], w_gate], axis=1).astype(BF16)
    ncv = 2 * NSA_KV * NSA_DH
    q, kcv, kv4, gl = _norm_proj(x2, gain, w_cat,
                                 [(nq, BF16, NSA_DH ** -0.5 * LOG2E), (ncv, BF16, 1.0), (nkv - ncv, BF16, 1.0),
                                  (NSA_KV * LANE, F32, 1.0)])
    kcv_t = kcv.reshape(batch, seq, 2, NSA_KV, NSA_DH).transpose(2, 0, 3, 1, 4)
    half = kcv_t.reshape(2, batch, NSA_KV, seq // CMP_STRIDE, CMP_STRIDE * NSA_DH)
    cmp = _nsa_compress(half,
                        jnp.stack([k_w1, v_w1]).astype(BF16),
                        jnp.stack([k_w2, v_w2]).astype(BF16),
                        jnp.stack([pos_k.reshape(1, -1), pos_v.reshape(1, -1)]))
    o = _nsa_attention(q, gl, kv4, cmp, batch, seq)
    return o, w_out.astype(BF16)


def kernel(x, norm_gains, final_norm, ff_w1, ff_w2, l0_sg_w_in, l0_sg_vnorm_g, l0_sg_vnorm_b, l0_sg_w_s, l0_sg_b_s, l0_sg_w_out, l1_ssm_w_in, l1_ssm_conv_w, l1_ssm_conv_b, l1_ssm_dt_bias, l1_ssm_a_log, l1_ssm_d_skip, l1_ssm_norm_g, l1_ssm_w_out, l2_nsa_w_in, l2_nsa_cmp_pos_k, l2_nsa_cmp_pos_v, l2_nsa_cmp_k_w1, l2_nsa_cmp_k_w2, l2_nsa_cmp_v_w1, l2_nsa_cmp_v_w2, l2_nsa_w_out, l3_sg_w_in, l3_sg_vnorm_g, l3_sg_vnorm_b, l3_sg_w_s, l3_sg_b_s, l3_sg_w_out):
    batch, seq, d = x.shape
    x2 = x.reshape(batch * seq, d)
    w1 = ff_w1.astype(BF16)
    w2 = ff_w2.astype(BF16)

    x2 = _gmlp_layer(x2, norm_gains[0, 0], l0_sg_w_in, l0_sg_vnorm_g, l0_sg_vnorm_b, l0_sg_w_s, l0_sg_b_s, l0_sg_w_out)
    x2 = _mlp(x2, norm_gains[0, 1], w1[0], w2[0])
    mix = _mamba_layer(x2, norm_gains[1, 0], l1_ssm_w_in, l1_ssm_conv_w, l1_ssm_conv_b, l1_ssm_dt_bias,
                       l1_ssm_a_log, l1_ssm_d_skip, l1_ssm_norm_g, l1_ssm_w_out, batch, seq)
    x2 = _mlp(x2, norm_gains[1, 1], w1[1], w2[1], mixer_proj=mix)
    mix = _nsa_layer(x2, norm_gains[2, 0], l2_nsa_w_in, l2_nsa_cmp_pos_k, l2_nsa_cmp_pos_v, l2_nsa_cmp_k_w1,
                     l2_nsa_cmp_k_w2, l2_nsa_cmp_v_w1, l2_nsa_cmp_v_w2, l2_nsa_w_out, batch, seq)
    x2 = _mlp(x2, norm_gains[2, 1], w1[2], w2[2], mixer_proj=mix)
    x2 = _gmlp_layer(x2, norm_gains[3, 0], l3_sg_w_in, l3_sg_vnorm_g, l3_sg_vnorm_b, l3_sg_w_s, l3_sg_b_s, l3_sg_w_out)
    x2 = _mlp(x2, norm_gains[3, 1], w1[3], w2[3], final_gain=final_norm)
    return x2.reshape(batch, seq, d)
```

```python
import functools

import numpy as np
import jax
import jax.numpy as jnp
from jax import lax
from jax.experimental import pallas as pl
from jax.experimental.pallas import tpu as pltpu

F32 = jnp.float32
BF16 = jnp.bfloat16

D_MODEL = 1024
D_FF = 4 * D_MODEL
NORM_EPS = 1e-5
NEG_INF = -1e30
FORCE = 1e6
SQRT_HALF = 0.7071067811865476
LOG2E = 1.4426950408889634

SG_CHUNK = 128
SG_WIDTH = 2 * D_MODEL
SG_GROUPS = 8
SG_GW = SG_WIDTH // SG_GROUPS

SSM_D_INNER = 2 * D_MODEL
SSM_HEAD_DIM = 64
SSM_HEADS = SSM_D_INNER // SSM_HEAD_DIM
SSM_GROUPS = 8
SSM_STATE = 128
SSM_CONV = 4
SSM_CHUNK = 128
SSM_GN = SSM_GROUPS * SSM_STATE
SSM_CONV_CH = SSM_D_INNER + 2 * SSM_GN
SSM_GW = SSM_D_INNER // SSM_GROUPS
SSM_HPG = SSM_HEADS // SSM_GROUPS

NSA_HEADS = 16
NSA_KV = 4
NSA_DH = 64
NSA_REP = NSA_HEADS // NSA_KV
CMP_BLOCK = 32
CMP_STRIDE = 16
CMP_HIDDEN = 4 * NSA_DH
SEL_BLOCK = 64
SEL_TOP_N = 16
WINDOW = 512
NSA_TQ = 128
NSA_TK = 256
NSA_WSPAN = WINDOW + NSA_TQ
NSA_VROWS = NSA_DH + 16

LANE = 128
VMEM_LIMIT = 56 * 1024 * 1024


def _params(sem):
    return pltpu.CompilerParams(dimension_semantics=sem, vmem_limit_bytes=VMEM_LIMIT)


def _resident(shape):
    nd = len(shape)
    return pl.BlockSpec(shape, lambda *_: (0,) * nd, pipeline_mode=pl.Buffered(1))


def _rms(x, g):
    return x * lax.rsqrt(jnp.mean(x * x, axis=-1, keepdims=True) + NORM_EPS) * g


def _gelu(a):
    return 0.5 * a * (1.0 + lax.erf(a * SQRT_HALF))


def _silu(a):
    return a * jax.nn.sigmoid(a)


def _dot(a, b):
    return jnp.dot(a, b, preferred_element_type=F32)


def _dot_nt(a, b):
    return lax.dot_general(a, b, (((1,), (1,)), ((), ())), preferred_element_type=F32)


def _dot_tn(a, b):
    return lax.dot_general(a, b, (((0,), (0,)), ((), ())), preferred_element_type=F32)


def _split3(v):
    hi = v.astype(BF16)
    r = v - hi.astype(F32)
    mid = r.astype(BF16)
    lo = (r - mid.astype(F32)).astype(BF16)
    return hi, mid, lo


def _norm_proj_kernel(x_ref, g_ref, w_ref, *out_refs, outs, tn):
    h = _rms(x_ref[...], g_ref[...]).astype(BF16)
    c0 = 0
    for o_ref, (width, _, scale) in zip(out_refs, outs):
        for c in range(0, width, tn):
            ce = min(c + tn, width)
            a = _dot(h, w_ref[:, c0 + c:c0 + ce])
            if scale != 1.0:
                a = a * scale
            o_ref[:, c:ce] = a.astype(o_ref.dtype)
        c0 += width


def _norm_proj(x2, gain, w, outs, tm=512, tn=512):
    n, d = x2.shape
    wtot = sum(o[0] for o in outs)
    assert w.shape == (d, wtot)
    return pl.pallas_call(
        functools.partial(_norm_proj_kernel, outs=tuple(outs), tn=tn),
        grid=(n // tm,),
        in_specs=[pl.BlockSpec((tm, d), lambda i: (i, 0)),
                  _resident((1, d)),
                  _resident((d, wtot))],
        out_specs=[pl.BlockSpec((tm, o[0]), lambda i: (i, 0)) for o in outs],
        out_shape=[jax.ShapeDtypeStruct((n, o[0]), o[1]) for o in outs],
        compiler_params=_params(("parallel",)),
        name="norm_proj",
    )(x2, gain.reshape(1, d), w)


def _mlp_kernel(*refs, tf, final, mixer_proj):
    if mixer_proj:
        a_ref, wo_ref, x_ref, g_ref, w1_ref, w2_ref, fg_ref, o_ref, h_ref, acc_ref = refs
        x = x_ref[...] + _dot(a_ref[...], wo_ref[...])
    else:
        x_ref, g_ref, w1_ref, w2_ref, fg_ref, o_ref, h_ref, acc_ref = refs
        x = x_ref[...]
    h_ref[...] = _rms(x, g_ref[...]).astype(BF16)
    acc_ref[...] = x
    for c in range(0, D_FF, tf):
        a = _dot(h_ref[...], w1_ref[:, c:c + tf])
        a = jnp.square(jnp.maximum(a, 0.0)).astype(BF16)
        acc_ref[...] += _dot(a, w2_ref[c:c + tf, :])
    y = acc_ref[...]
    if final:
        y = _rms(y, fg_ref[...])
    o_ref[...] = y


def _mlp(x2, gain, w1, w2, final_gain=None, mixer_proj=None, tm=512, tf=512):
    n, d = x2.shape
    final = final_gain is not None
    fg = (final_gain if final else gain).reshape(1, d)
    in_specs = [pl.BlockSpec((tm, d), lambda i: (i, 0)),
                _resident((1, d)),
                _resident((d, D_FF)),
                _resident((D_FF, d)),
                _resident((1, d))]
    args = (x2, gain.reshape(1, d), w1, w2, fg)
    if mixer_proj is not None:
        a, w_out = mixer_proj
        k = a.shape[1]
        in_specs = [pl.BlockSpec((tm, k), lambda i: (i, 0)), _resident((k, d))] + in_specs
        args = (a, w_out) + args
    return pl.pallas_call(
        functools.partial(_mlp_kernel, tf=tf, final=final, mixer_proj=mixer_proj is not None),
        grid=(n // tm,),
        in_specs=in_specs,
        out_specs=pl.BlockSpec((tm, d), lambda i: (i, 0)),
        out_shape=jax.ShapeDtypeStruct((n, d), F32),
        scratch_shapes=[pltpu.VMEM((tm, d), BF16), pltpu.VMEM((tm, d), F32)],
        compiler_params=_params(("parallel",)),
        name="sqrelu_mlp",
    )(*args)


def _gmlp_kernel(x_ref, g_ref, win_ref, lng_ref, lnb_ref, ws_ref, bs_ref, wout_ref, o_ref,
                 h_ref, u_ref, v_ref, vb_ref, gt_ref, *, tm, tn, sub):
    row = lax.broadcasted_iota(jnp.int32, (SG_CHUNK, SG_CHUNK), 0)
    col = lax.broadcasted_iota(jnp.int32, (SG_CHUNK, SG_CHUNK), 1)
    causal = col <= row
    for s0 in range(0, tm, sub):
        ss = slice(s0, s0 + sub)
        x = x_ref[ss, :]
        h_ref[ss, :] = _rms(x, g_ref[...]).astype(BF16)
        for c in range(0, 2 * SG_WIDTH, tn):
            a = _gelu(_dot(h_ref[ss, :], win_ref[:, c:c + tn]))
            if c < SG_WIDTH:
                u_ref[ss, c:c + tn] = a.astype(BF16)
            else:
                v_ref[ss, c - SG_WIDTH:c - SG_WIDTH + tn] = a
        v = v_ref[ss, :]
        mu = jnp.mean(v, axis=-1, keepdims=True)
        dv = v - mu
        var = jnp.mean(dv * dv, axis=-1, keepdims=True)
        vb_ref[ss, :] = (dv * lax.rsqrt(var + NORM_EPS) * lng_ref[...] + lnb_ref[...]).astype(BF16)
        for g in range(SG_GROUPS):
            wg = jnp.where(causal, ws_ref[g], 0.0).astype(BF16)
            bg = bs_ref[:, g:g + 1]
            cs = slice(g * SG_GW, (g + 1) * SG_GW)
            for r0 in range(s0, s0 + sub, SG_CHUNK):
                rs = slice(r0, r0 + SG_CHUNK)
                sv = _dot(wg, vb_ref[rs, cs]) + bg
                gt_ref[rs, cs] = (u_ref[rs, cs].astype(F32) * sv).astype(BF16)
        o_ref[ss, :] = x + _dot(gt_ref[ss, :], wout_ref[...])


def _gmlp_layer(x2, gain, w_in, vnorm_g, vnorm_b, w_s, b_s, w_out, tm=512, tn=512, sub=256):
    n, d = x2.shape
    return pl.pallas_call(
        functools.partial(_gmlp_kernel, tm=tm, tn=tn, sub=sub),
        grid=(n // tm,),
        in_specs=[pl.BlockSpec((tm, d), lambda i: (i, 0)),
                  _resident((1, d)),
                  _resident((d, 2 * SG_WIDTH)),
                  _resident((1, SG_WIDTH)),
                  _resident((1, SG_WIDTH)),
                  _resident((SG_GROUPS, SG_CHUNK, SG_CHUNK)),
                  _resident((SG_CHUNK, SG_GROUPS)),
                  _resident((SG_WIDTH, d))],
        out_specs=pl.BlockSpec((tm, d), lambda i: (i, 0)),
        out_shape=jax.ShapeDtypeStruct((n, d), F32),
        scratch_shapes=[pltpu.VMEM((tm, d), BF16),
                        pltpu.VMEM((tm, SG_WIDTH), BF16),
                        pltpu.VMEM((tm, SG_WIDTH), F32),
                        pltpu.VMEM((tm, SG_WIDTH), BF16),
                        pltpu.VMEM((tm, SG_WIDTH), BF16)],
        compiler_params=_params(("parallel",)),
        name="gmlp_mixer",
    )(x2, gain.reshape(1, d), w_in.astype(BF16), vnorm_g.reshape(1, -1), vnorm_b.reshape(1, -1),
      w_s, b_s.T, w_out.astype(BF16))


def _ssd_kernel(z_ref, xbc_ref, dt_ref, cw_ref, cb_ref, dtb_ref, alog_ref, dsk_ref, ng_ref,
                tril_ref, e_ref, o_ref, xpad_ref, st_ref, xs_ref, b_ref, c_ref, *, cw):
    L = SSM_CHUNK
    hist = 8

    @pl.when(pl.program_id(1) == 0)
    def _():
        xpad_ref[0:hist, :] = jnp.zeros((hist, SSM_CONV_CH), F32)
        st_ref[...] = jnp.zeros_like(st_ref)

    for j in range(0, SSM_CONV_CH, cw):
        js = slice(j, j + cw)
        xpad_ref[hist:hist + L, js] = xbc_ref[:, js].astype(F32)
        acc = cb_ref[:, js] + cw_ref[SSM_CONV - 1:SSM_CONV, js] * xpad_ref[hist:hist + L, js]
        for k in range(1, SSM_CONV):
            acc = acc + cw_ref[SSM_CONV - 1 - k:SSM_CONV - k, js] * xpad_ref[hist - k:hist - k + L, js]
        xpad_ref[0:hist, js] = xpad_ref[L:L + hist, js]
        act = _silu(acc)
        if j < SSM_D_INNER:
            xs_ref[:, js] = act
        elif j < SSM_D_INNER + SSM_GN:
            b_ref[:, j - SSM_D_INNER:j - SSM_D_INNER + cw] = act.astype(BF16)
        else:
            o0 = j - SSM_D_INNER - SSM_GN
            c_ref[:, o0:o0 + cw] = act.astype(BF16)

    dtr = dt_ref[...] + dtb_ref[...]
    dt = jnp.maximum(dtr, 0.0) + jnp.log1p(jnp.exp(-jnp.abs(dtr)))
    da = dt * (-jnp.exp(alog_ref[...]))
    tril = tril_ref[...]
    cs = sum(_dot(tril, p) for p in _split3(da))
    cs_t = cs.T
    dt3 = _split3(dt)
    cs3 = _split3(cs)
    row = lax.broadcasted_iota(jnp.int32, (L, L), 0)
    col = lax.broadcasted_iota(jnp.int32, (L, L), 1)
    causal = col <= row

    for g in range(SSM_GROUPS):
        sl = slice(g * SSM_GW, (g + 1) * SSM_GW)
        eg = e_ref[:, sl]
        dt_e = sum(_dot(p, eg) for p in dt3)
        cs_e = sum(_dot(p, eg) for p in cs3)
        xs_g = xs_ref[:, sl]
        xdt = xs_g * dt_e
        xdt_b = xdt.astype(BF16)
        cs_last = cs_e[L - 1:L, :]
        xw_b = (xdt * jnp.exp(cs_last - cs_e)).astype(BF16)
        bg = b_ref[:, g * SSM_STATE:(g + 1) * SSM_STATE]
        cg = c_ref[:, g * SSM_STATE:(g + 1) * SSM_STATE]
        cb = _dot_nt(cg, bg)
        st_prev = st_ref[g]
        y_off = _dot(cg, st_prev.astype(BF16)) * jnp.exp(cs_e)
        st_ref[g] = st_prev * jnp.exp(cs_last) + _dot_tn(bg, xw_b)
        ys = []
        for r in range(SSM_HPG):
            h = g * SSM_HPG + r
            seg = cs[:, h:h + 1] - cs_t[h:h + 1, :]
            dec = jnp.exp(jnp.where(causal, seg, NEG_INF))
            m = (cb * dec).astype(BF16)
            ys.append(_dot(m, xdt_b[:, r * SSM_HEAD_DIM:(r + 1) * SSM_HEAD_DIM]))
        y = jnp.concatenate(ys, axis=1) + y_off + dsk_ref[:, sl] * xs_g
        yz = y * _silu(z_ref[:, sl].astype(F32))
        ms = jnp.mean(yz * yz, axis=-1, keepdims=True)
        o_ref[:, sl] = (yz * lax.rsqrt(ms + NORM_EPS) * ng_ref[:, sl]).astype(BF16)


def _ssd_core(z, xbc, dtp, conv_w, conv_b, dt_bias, a_log, d_skip, norm_g, batch, seq):
    L = SSM_CHUNK
    nc = seq // L
    tril = jnp.asarray(np.tril(np.ones((L, L), np.float32)), BF16)
    e_np = np.zeros((LANE, SSM_D_INNER), np.float32)
    for h in range(SSM_HEADS):
        e_np[h, h * SSM_HEAD_DIM:(h + 1) * SSM_HEAD_DIM] = 1.0
    expand = jnp.asarray(e_np, BF16)
    pad = LANE - SSM_HEADS
    dtb = jnp.pad(dt_bias.astype(F32), (0, pad)).reshape(1, LANE)
    alog = jnp.pad(a_log.astype(F32), (0, pad)).reshape(1, LANE)
    dsk = jnp.repeat(d_skip.astype(F32), SSM_HEAD_DIM).reshape(1, SSM_D_INNER)
    row = lambda b, c: (b * nc + c, 0)
    return pl.pallas_call(
        functools.partial(_ssd_kernel, cw=512),
        grid=(batch, nc),
        in_specs=[pl.BlockSpec((L, SSM_D_INNER), row),
                  pl.BlockSpec((L, SSM_CONV_CH), row),
                  pl.BlockSpec((L, LANE), row),
                  _resident((SSM_CONV, SSM_CONV_CH)),
                  _resident((1, SSM_CONV_CH)),
                  _resident((1, LANE)),
                  _resident((1, LANE)),
                  _resident((1, SSM_D_INNER)),
                  _resident((1, SSM_D_INNER)),
                  _resident((L, L)),
                  _resident((LANE, SSM_D_INNER))],
        out_specs=pl.BlockSpec((L, SSM_D_INNER), row),
        out_shape=jax.ShapeDtypeStruct((batch * seq, SSM_D_INNER), BF16),
        scratch_shapes=[pltpu.VMEM((L + 8, SSM_CONV_CH), F32),
                        pltpu.VMEM((SSM_GROUPS, SSM_STATE, SSM_GW), F32),
                        pltpu.VMEM((L, SSM_D_INNER), F32),
                        pltpu.VMEM((L, SSM_GN), BF16),
                        pltpu.VMEM((L, SSM_GN), BF16)],
        compiler_params=_params(("arbitrary", "arbitrary")),
        name="ssd_core",
    )(z, xbc, dtp, conv_w, conv_b.reshape(1, -1), dtb, alog, dsk, norm_g.reshape(1, -1), tril, expand)


def _mamba_layer(x2, gain, w_in, conv_w, conv_b, dt_bias, a_log, d_skip, norm_g, w_out, batch, seq):
    d = x2.shape[1]
    nzx = SSM_D_INNER + SSM_CONV_CH
    w_dt = jnp.pad(w_in[:, nzx:], ((0, 0), (0, LANE - SSM_HEADS)))
    w_cat = jnp.concatenate([w_in[:, :nzx], w_dt], axis=1).astype(BF16)
    z, xbc, dtp = _norm_proj(x2, gain, w_cat,
                             [(SSM_D_INNER, BF16, 1.0), (SSM_CONV_CH, BF16, 1.0), (LANE, F32, 1.0)])
    yn = _ssd_core(z, xbc, dtp, conv_w, conv_b, dt_bias, a_log, d_skip, norm_g, batch, seq)
    return yn, w_out.astype(BF16)


def _cmp_kernel(x_ref, w1_ref, w2_ref, pos_ref, o_ref):
    DH = NSA_DH
    nblk = x_ref.shape[0] // CMP_STRIDE
    npair = x_ref.shape[1] // DH
    posb = _dot(jnp.broadcast_to(pos_ref[...], (8, CMP_BLOCK * DH)).astype(BF16), w1_ref[...])[0:1, :]
    a = [jnp.zeros((nblk, CMP_HIDDEN), F32) for _ in range(npair)]
    b = [jnp.zeros((nblk, CMP_HIDDEN), F32) for _ in range(npair)]
    for l in range(CMP_STRIDE):
        xl = x_ref[pl.ds(l, nblk, stride=CMP_STRIDE), :].astype(BF16)
        for g in range(npair):
            xg = xl[:, g * DH:(g + 1) * DH]
            a[g] = a[g] + _dot(xg, w1_ref[l * DH:(l + 1) * DH, :])
            b[g] = b[g] + _dot(xg, w1_ref[(CMP_STRIDE + l) * DH:(CMP_STRIDE + l + 1) * DH, :])
    for g in range(npair):
        pre = a[g] + pltpu.roll(b[g], nblk - 1, 0) + posb
        o_ref[g] = _dot(_gelu(pre).astype(BF16), w2_ref[...]).astype(o_ref.dtype)


def _nsa_compress(kcv, w1, w2, pos, batch, seq):
    nblk = seq // CMP_STRIDE
    npair = LANE // NSA_DH
    slabs = NSA_KV // npair
    return pl.pallas_call(
        _cmp_kernel,
        grid=(batch, 2 * slabs),
        in_specs=[pl.BlockSpec((seq, LANE), lambda b, j: (b, j)),
                  pl.BlockSpec((None,) + w1.shape[1:], lambda b, j: (j // slabs, 0, 0)),
                  pl.BlockSpec((None,) + w2.shape[1:], lambda b, j: (j // slabs, 0, 0)),
                  pl.BlockSpec((None,) + pos.shape[1:], lambda b, j: (j // slabs, 0, 0))],
        out_specs=pl.BlockSpec((None, None, npair, nblk, NSA_DH), lambda b, j: (j // slabs, b, j % slabs, 0, 0)),
        out_shape=jax.ShapeDtypeStruct((2, batch, NSA_KV, nblk, NSA_DH), BF16),
        compiler_params=_params(("parallel", "parallel")),
        name="nsa_compress",
    )(kcv, w1, w2, pos)


def _nsa_attn_kernel(q_ref, gl_ref, kv_ref, kc_ref, vc_ref, ovt_ref, ind_ref,
                     o_ref, ksp_ref, kws_ref, vst_ref, vwt_ref, vct_ref, m_ref, acc_ref):
    R, TQ, TK, DH, NG = NSA_REP, NSA_TQ, NSA_TK, NSA_DH, NSA_KV
    W = R * TQ
    qi = pl.program_id(1)
    t0 = pl.multiple_of(qi * TQ, TQ)
    seq = kv_ref.shape[0]
    ncp = kc_ref.shape[1]
    nsel = seq // SEL_BLOCK
    groups = range(NG)

    @pl.when(qi == 0)
    def _():
        def part(kind, a):
            c0 = (kind * NG + a) * DH
            return kv_ref[:, c0:c0 + DH]

        def t_ones(v):
            n = v.shape[0]
            one = jnp.where(lax.broadcasted_iota(jnp.int32, (n, LANE - DH), 1) == 0, 1.0, 0.0)
            vp = jnp.concatenate([v.astype(F32), one], axis=1)
            vt = jnp.concatenate([vp[j:j + LANE].T for j in range(0, n, LANE)], axis=1)
            return vt[0:NSA_VROWS].astype(BF16)
        for a in groups:
            ksp_ref[a] = jnp.concatenate([part(0, a), ind_ref[...]], axis=1)
            kws_ref[a] = part(2, a)
            vst_ref[a] = t_ones(part(1, a))
            vwt_ref[a] = t_ones(part(3, a))
            vct_ref[a] = t_ones(vc_ref[a])

    q_heads = [[q_ref[:, (a * R + r) * DH:(a * R + r + 1) * DH] for r in range(R)] for a in groups]
    qs = [jnp.concatenate(q_heads[a], axis=0) for a in groups]
    t_lane = t0 + lax.broadcasted_iota(jnp.int32, (1, TQ), 1)

    def heads(x):
        return jnp.concatenate([x] * R, axis=1)

    def colmax(s):
        return jnp.max(s, axis=0, keepdims=True)

    k0w = pl.multiple_of(jnp.maximum(t0 - WINDOW, 0), TQ)
    kpos = k0w + lax.broadcasted_iota(jnp.int32, (NSA_WSPAN, TQ), 0)
    bias_w = heads(jnp.where((kpos <= t_lane) & (kpos > t_lane - WINDOW), 0.0, NEG_INF))
    sw = [_dot_nt(kws_ref[a, pl.ds(k0w, NSA_WSPAN), :], qs[a]) + bias_w for a in groups]
    aw = [_dot(vwt_ref[a, :, pl.ds(k0w, NSA_WSPAN)], jnp.exp2(sw[a] - colmax(sw[a])).astype(BF16)) for a in groups]

    n_sub = lax.broadcasted_iota(jnp.int32, (ncp, TQ), 0)
    valid_c = (n_sub * CMP_STRIDE + (CMP_BLOCK - 1)) <= t_lane
    bias_c = heads(jnp.where(valid_c, 0.0, NEG_INF))
    keep_c = heads(jnp.where(valid_c, 1.0, 0.0))
    ovt = ovt_ref[...]
    sc = [_dot_nt(kc_ref[a], qs[a]) + bias_c for a in groups]
    oc, imp_t = [], []
    for a in groups:
        e = jnp.exp2(sc[a] - colmax(sc[a])) * keep_c
        parts = _split3(e)[:2]
        u = sum(_dot(ovt, p) for p in parts)
        l_t = u[nsel:nsel + 1, :]
        un = u[0:nsel, :] / jnp.where(l_t > 0.0, l_t, 1.0)
        imp = un[:, 0:TQ]
        for r in range(1, R):
            imp = imp + un[:, r * TQ:(r + 1) * TQ]
        imp_t.append(imp)
        oc.append(_dot(vct_ref[a], parts[0]))

    j_i = lax.broadcasted_iota(jnp.int32, (nsel, TQ), 0)
    t_l = t0 + lax.broadcasted_iota(jnp.int32, (nsel, TQ), 1)
    cur = lax.shift_right_logical(t_l, 6)
    forced = (j_i == 0) | (j_i == cur) | (j_i == cur - 1)
    future = j_i * SEL_BLOCK > t_l
    val = [jnp.where(forced, FORCE, jnp.where(future, -FORCE, imp_t[a])) for a in groups]
    cnt = [jnp.zeros((nsel, TQ), F32) for _ in groups]
    for i in range(nsel):
        for a in groups:
            vi = val[a][i:i + 1, :]
            ge = jnp.where(vi >= val[a], 1.0, 0.0)
            gt = jnp.where(vi > val[a], 1.0, 0.0)
            cnt[a] = cnt[a] + jnp.where(j_i > i, ge, gt)
    in_loop = j_i < qi * (TQ // SEL_BLOCK)
    qsp = []
    for a in groups:
        sb_t = jnp.where((cnt[a] < float(SEL_TOP_N)) & in_loop, 0.0, NEG_INF)
        sb = jnp.concatenate([sb_t, jnp.zeros((LANE - nsel, TQ), F32)], axis=0).T
        sbb = sb[:, :DH].astype(BF16)
        qsp.append(jnp.concatenate([jnp.concatenate([qh, sbb], axis=1) for qh in q_heads[a]], axis=0))

    k_sub = lax.broadcasted_iota(jnp.int32, (TQ, TQ), 0)
    i_lane = lax.broadcasted_iota(jnp.int32, (TQ, TQ), 1)
    causal_b = heads(jnp.where(k_sub <= i_lane, 0.0, NEG_INF))
    sd = [_dot_nt(ksp_ref[a, pl.ds(t0, TQ), 0:DH], qs[a]) + causal_b for a in groups]
    for a in groups:
        m = colmax(sd[a])
        m_ref[a] = m
        acc_ref[a] = _dot(vst_ref[a, :, pl.ds(t0, TQ)], jnp.exp2(sd[a] - m).astype(BF16))

    gates_t = [jax.nn.sigmoid(gl_ref[:, a * LANE:(a + 1) * LANE]).T for a in groups]

    def gate_row(a, b):
        return jnp.concatenate([gates_t[a][3 * r + b:3 * r + b + 1, :] for r in range(R)], axis=1)

    part = []
    for a in groups:
        l_c = oc[a][DH:DH + 1, :]
        w_c = gate_row(a, 0) / jnp.where(l_c > 0.0, l_c, 1.0)
        w_w = gate_row(a, 2) / aw[a][DH:DH + 1, :]
        part.append(w_c * oc[a][0:DH, :] + w_w * aw[a][0:DH, :])

    def sel_tile(kt, carry):
        k0 = pl.multiple_of(kt * TK, TK)
        s_all = [_dot_nt(ksp_ref[a, pl.ds(k0, TK), :], qsp[a]) for a in groups]
        for a in groups:
            m_old = m_ref[a]
            m_new = jnp.maximum(m_old, colmax(s_all[a]))
            m_ref[a] = m_new
            pv = _dot(vst_ref[a, :, pl.ds(k0, TK)], jnp.exp2(s_all[a] - m_new).astype(BF16))
            acc_ref[a] = jnp.exp2(m_old - m_new) * acc_ref[a] + pv
        return carry

    tq_per_tk = TK // TQ
    lax.fori_loop(0, lax.shift_right_logical(qi + tq_per_tk - 1, tq_per_tk.bit_length() - 1), sel_tile, 0)

    cols = []
    for a in groups:
        acc = acc_ref[a]
        o_t = part[a] + (gate_row(a, 1) / acc[DH:DH + 1, :]) * acc[0:DH, :]
        for r in range(0, R, 2):
            pair = jnp.concatenate([o_t[:, r * TQ:(r + 1) * TQ], o_t[:, (r + 1) * TQ:(r + 2) * TQ]], axis=0)
            cols.append(pair.T)
    o_ref[...] = jnp.concatenate(cols, axis=1).astype(o_ref.dtype)


def _nsa_attention(q, gl, kv4, cmp, batch, seq):
    R, TQ, DH, G = NSA_REP, NSA_TQ, NSA_DH, NSA_KV
    nq = seq // TQ
    ncp = cmp.shape[3]
    nsel = seq // SEL_BLOCK
    nc = (seq - CMP_BLOCK) // CMP_STRIDE + 1
    assert ncp == LANE and nsel < DH and NSA_TK % LANE == 0 and TQ % SEL_BLOCK == 0 and 2 * DH == LANE and R % 2 == 0
    ov = np.zeros((DH, ncp), np.float32)
    for n in range(nc):
        for j in range(nsel):
            if n * CMP_STRIDE <= j * SEL_BLOCK + SEL_BLOCK - 1 and n * CMP_STRIDE + CMP_BLOCK - 1 >= j * SEL_BLOCK:
                ov[j, n] = 1.0
    ov[nsel, :] = 1.0
    ind = np.zeros((seq, DH), np.float32)
    for j in range(nsel):
        ind[j * SEL_BLOCK:(j + 1) * SEL_BLOCK, j] = 1.0
    cmp_spec = lambda idx: pl.BlockSpec((None, None, G, ncp, DH), lambda b, i: (idx, b, 0, 0, 0))
    return pl.pallas_call(
        _nsa_attn_kernel,
        grid=(batch, nq),
        in_specs=[pl.BlockSpec((TQ, G * R * DH), lambda b, i: (b * nq + i, 0)),
                  pl.BlockSpec((TQ, G * LANE), lambda b, i: (b * nq + i, 0)),
                  pl.BlockSpec((seq, 4 * G * DH), lambda b, i: (b, 0)),
                  cmp_spec(0), cmp_spec(1),
                  _resident((DH, ncp)),
                  _resident((seq, DH))],
        out_specs=pl.BlockSpec((TQ, G * R * DH), lambda b, i: (b * nq + i, 0)),
        out_shape=jax.ShapeDtypeStruct((batch * seq, NSA_HEADS * DH), BF16),
        scratch_shapes=[pltpu.VMEM((G, seq, 2 * DH), BF16),
                        pltpu.VMEM((G, seq, DH), BF16),
                        pltpu.VMEM((G, NSA_VROWS, seq), BF16),
                        pltpu.VMEM((G, NSA_VROWS, seq), BF16),
                        pltpu.VMEM((G, NSA_VROWS, ncp), BF16),
                        pltpu.VMEM((G, 1, R * TQ), F32),
                        pltpu.VMEM((G, NSA_VROWS, R * TQ), F32)],
        compiler_params=_params(("arbitrary", "arbitrary")),
        name="nsa_attention",
    )(q, gl, kv4, cmp, cmp, jnp.asarray(ov, BF16), jnp.asarray(ind, BF16))


def _nsa_layer(x2, gain, w_in, pos_k, pos_v, k_w1, k_w2, v_w1, v_w2, w_out, batch, seq):
    d = x2.shape[1]
    nq = NSA_HEADS * NSA_DH
    nkv = 6 * NSA_KV * NSA_DH
    ngate = 3 * NSA_REP
    w_gate = w_in[:, nq + nkv:].reshape(d, NSA_KV, ngate)
    w_gate = jnp.pad(w_gate, ((0, 0), (0, 0), (0, LANE - ngate))).reshape(d, NSA_KV * LANE)
    w_cat = jnp.concatenate([w_in[:, :nq + nkv], w_gate], axis=1).astype(BF16)
    ncv = 2 * NSA_KV * NSA_DH
    q, kcv, kv4, gl = _norm_proj(x2, gain, w_cat,
                                 [(nq, BF16, NSA_DH ** -0.5 * LOG2E), (ncv, F32, 1.0), (nkv - ncv, BF16, 1.0),
                                  (NSA_KV * LANE, F32, 1.0)])
    cmp = _nsa_compress(kcv,
                        jnp.stack([k_w1, v_w1]).astype(BF16),
                        jnp.stack([k_w2, v_w2]).astype(BF16),
                        jnp.stack([pos_k.reshape(1, -1), pos_v.reshape(1, -1)]), batch, seq)
    o = _nsa_attention(q, gl, kv4, cmp, batch, seq)
    return o, w_out.astype(BF16)


def kernel(x, norm_gains, final_norm, ff_w1, ff_w2, l0_sg_w_in, l0_sg_vnorm_g, l0_sg_vnorm_b, l0_sg_w_s, l0_sg_b_s, l0_sg_w_out, l1_ssm_w_in, l1_ssm_conv_w, l1_ssm_conv_b, l1_ssm_dt_bias, l1_ssm_a_log, l1_ssm_d_skip, l1_ssm_norm_g, l1_ssm_w_out, l2_nsa_w_in, l2_nsa_cmp_pos_k, l2_nsa_cmp_pos_v, l2_nsa_cmp_k_w1, l2_nsa_cmp_k_w2, l2_nsa_cmp_v_w1, l2_nsa_cmp_v_w2, l2_nsa_w_out, l3_sg_w_in, l3_sg_vnorm_g, l3_sg_vnorm_b, l3_sg_w_s, l3_sg_b_s, l3_sg_w_out):
    batch, seq, d = x.shape
    x2 = x.reshape(batch * seq, d)
    w1 = ff_w1.astype(BF16)
    w2 = ff_w2.astype(BF16)

    x2 = _gmlp_layer(x2, norm_gains[0, 0], l0_sg_w_in, l0_sg_vnorm_g, l0_sg_vnorm_b, l0_sg_w_s, l0_sg_b_s, l0_sg_w_out)
    x2 = _mlp(x2, norm_gains[0, 1], w1[0], w2[0])
    mix = _mamba_layer(x2, norm_gains[1, 0], l1_ssm_w_in, l1_ssm_conv_w, l1_ssm_conv_b, l1_ssm_dt_bias,
                       l1_ssm_a_log, l1_ssm_d_skip, l1_ssm_norm_g, l1_ssm_w_out, batch, seq)
    x2 = _mlp(x2, norm_gains[1, 1], w1[1], w2[1], mixer_proj=mix)
    mix = _nsa_layer(x2, norm_gains[2, 0], l2_nsa_w_in, l2_nsa_cmp_pos_k, l2_nsa_cmp_pos_v, l2_nsa_cmp_k_w1,
                     l2_nsa_cmp_k_w2, l2_nsa_cmp_v_w1, l2_nsa_cmp_v_w2, l2_nsa_w_out, batch, seq)
    x2 = _mlp(x2, norm_gains[2, 1], w1[2], w2[2], mixer_proj=mix)
    x2 = _gmlp_layer(x2, norm_gains[3, 0], l3_sg_w_in, l3_sg_vnorm_g, l3_sg_vnorm_b, l3_sg_w_s, l3_sg_b_s, l3_sg_w_out)
    x2 = _mlp(x2, norm_gains[3, 1], w1[3], w2[3], final_gain=final_norm)
    return x2.reshape(batch, seq, d)
```

```python
import functools

import numpy as np
import jax
import jax.numpy as jnp
from jax import lax
from jax.experimental import pallas as pl
from jax.experimental.pallas import tpu as pltpu

F32 = jnp.float32
BF16 = jnp.bfloat16

D_MODEL = 1024
D_FF = 4 * D_MODEL
NORM_EPS = 1e-5
NEG_INF = -1e30
FORCE = 1e6
SQRT_HALF = 0.7071067811865476
LOG2E = 1.4426950408889634

SG_CHUNK = 128
SG_WIDTH = 2 * D_MODEL
SG_GROUPS = 8
SG_GW = SG_WIDTH // SG_GROUPS

SSM_D_INNER = 2 * D_MODEL
SSM_HEAD_DIM = 64
SSM_HEADS = SSM_D_INNER // SSM_HEAD_DIM
SSM_GROUPS = 8
SSM_STATE = 128
SSM_CONV = 4
SSM_CHUNK = 128
SSM_SUB = 4
SSM_GN = SSM_GROUPS * SSM_STATE
SSM_CONV_CH = SSM_D_INNER + 2 * SSM_GN
SSM_GW = SSM_D_INNER // SSM_GROUPS
SSM_HPG = SSM_HEADS // SSM_GROUPS

NSA_HEADS = 16
NSA_KV = 4
NSA_DH = 64
NSA_REP = NSA_HEADS // NSA_KV
CMP_BLOCK = 32
CMP_STRIDE = 16
CMP_HIDDEN = 4 * NSA_DH
SEL_BLOCK = 64
SEL_TOP_N = 16
WINDOW = 512
NSA_TQ = 128
NSA_TK = 256
NSA_WSPAN = WINDOW + NSA_TQ
NSA_VROWS = NSA_DH + 16

LANE = 128
VMEM_LIMIT = 56 * 1024 * 1024


def _params(sem):
    return pltpu.CompilerParams(dimension_semantics=sem, vmem_limit_bytes=VMEM_LIMIT)


def _resident(shape):
    nd = len(shape)
    return pl.BlockSpec(shape, lambda *_: (0,) * nd, pipeline_mode=pl.Buffered(1))


def _rms(x, g):
    return x * lax.rsqrt(jnp.mean(x * x, axis=-1, keepdims=True) + NORM_EPS) * g


def _gelu(a):
    return 0.5 * a * (1.0 + lax.erf(a * SQRT_HALF))


def _silu(a):
    return a * jax.nn.sigmoid(a)


def _dot(a, b):
    return jnp.dot(a, b, preferred_element_type=F32)


def _dot_nt(a, b):
    return lax.dot_general(a, b, (((1,), (1,)), ((), ())), preferred_element_type=F32)


def _dot_tn(a, b):
    return lax.dot_general(a, b, (((0,), (0,)), ((), ())), preferred_element_type=F32)


def _split3(v):
    hi = v.astype(BF16)
    r = v - hi.astype(F32)
    mid = r.astype(BF16)
    lo = (r - mid.astype(F32)).astype(BF16)
    return hi, mid, lo


def _norm_proj_kernel(x_ref, g_ref, w_ref, *out_refs, outs, tn):
    h = _rms(x_ref[...], g_ref[...]).astype(BF16)
    c0 = 0
    for o_ref, (width, _, scale) in zip(out_refs, outs):
        for c in range(0, width, tn):
            ce = min(c + tn, width)
            a = _dot(h, w_ref[:, c0 + c:c0 + ce])
            if scale != 1.0:
                a = a * scale
            o_ref[:, c:ce] = a.astype(o_ref.dtype)
        c0 += width


def _norm_proj(x2, gain, w, outs, tm=512, tn=512):
    n, d = x2.shape
    wtot = sum(o[0] for o in outs)
    assert w.shape == (d, wtot)
    return pl.pallas_call(
        functools.partial(_norm_proj_kernel, outs=tuple(outs), tn=tn),
        grid=(n // tm,),
        in_specs=[pl.BlockSpec((tm, d), lambda i: (i, 0)),
                  _resident((1, d)),
                  _resident((d, wtot))],
        out_specs=[pl.BlockSpec((tm, o[0]), lambda i: (i, 0)) for o in outs],
        out_shape=[jax.ShapeDtypeStruct((n, o[0]), o[1]) for o in outs],
        compiler_params=_params(("parallel",)),
        name="norm_proj",
    )(x2, gain.reshape(1, d), w)


def _mlp_kernel(*refs, tf, final, mixer_proj):
    if mixer_proj:
        a_ref, wo_ref, x_ref, g_ref, w1_ref, w2_ref, fg_ref, o_ref, h_ref, acc_ref = refs
        x = x_ref[...] + _dot(a_ref[...], wo_ref[...])
    else:
        x_ref, g_ref, w1_ref, w2_ref, fg_ref, o_ref, h_ref, acc_ref = refs
        x = x_ref[...]
    h_ref[...] = _rms(x, g_ref[...]).astype(BF16)
    acc_ref[...] = x
    for c in range(0, D_FF, tf):
        a = _dot(h_ref[...], w1_ref[:, c:c + tf])
        a = jnp.square(jnp.maximum(a, 0.0)).astype(BF16)
        acc_ref[...] += _dot(a, w2_ref[c:c + tf, :])
    y = acc_ref[...]
    if final:
        y = _rms(y, fg_ref[...])
    o_ref[...] = y


def _mlp(x2, gain, w1, w2, final_gain=None, mixer_proj=None, tm=512, tf=512):
    n, d = x2.shape
    final = final_gain is not None
    fg = (final_gain if final else gain).reshape(1, d)
    in_specs = [pl.BlockSpec((tm, d), lambda i: (i, 0)),
                _resident((1, d)),
                _resident((d, D_FF)),
                _resident((D_FF, d)),
                _resident((1, d))]
    args = (x2, gain.reshape(1, d), w1, w2, fg)
    if mixer_proj is not None:
        a, w_out = mixer_proj
        k = a.shape[1]
        in_specs = [pl.BlockSpec((tm, k), lambda i: (i, 0)), _resident((k, d))] + in_specs
        args = (a, w_out) + args
    return pl.pallas_call(
        functools.partial(_mlp_kernel, tf=tf, final=final, mixer_proj=mixer_proj is not None),
        grid=(n // tm,),
        in_specs=in_specs,
        out_specs=pl.BlockSpec((tm, d), lambda i: (i, 0)),
        out_shape=jax.ShapeDtypeStruct((n, d), F32),
        scratch_shapes=[pltpu.VMEM((tm, d), BF16), pltpu.VMEM((tm, d), F32)],
        compiler_params=_params(("parallel",)),
        name="sqrelu_mlp",
    )(*args)


def _gmlp_kernel(x_ref, g_ref, win_ref, lng_ref, lnb_ref, ws_ref, bs_ref, wout_ref, o_ref,
                 h_ref, u_ref, v_ref, vb_ref, gt_ref, *, tm, tn, sub):
    assert tm == 2 * sub
    slab = sub // (2 * SG_WIDTH // tn)
    row = lax.broadcasted_iota(jnp.int32, (SG_CHUNK, SG_CHUNK), 0)
    col = lax.broadcasted_iota(jnp.int32, (SG_CHUNK, SG_CHUNK), 1)
    causal = col <= row

    def project(s0, c):
        ss = slice(s0, s0 + sub)
        a = _gelu(_dot(h_ref[ss, :], win_ref[:, c:c + tn]))
        if c < SG_WIDTH:
            u_ref[ss, c:c + tn] = a.astype(BF16)
        else:
            v_ref[ss, c - SG_WIDTH:c - SG_WIDTH + tn] = a

    def layernorm(r0):
        rs = slice(r0, r0 + slab)
        v = v_ref[rs, :]
        mu = jnp.mean(v, axis=-1, keepdims=True)
        dv = v - mu
        var = jnp.mean(dv * dv, axis=-1, keepdims=True)
        vb_ref[rs, :] = (dv * lax.rsqrt(var + NORM_EPS) * lng_ref[...] + lnb_ref[...]).astype(BF16)

    def gate(s0, g):
        wg = jnp.where(causal, ws_ref[g], 0.0).astype(BF16)
        bg = bs_ref[:, g:g + 1]
        cs = slice(g * SG_GW, (g + 1) * SG_GW)
        for r0 in range(s0, s0 + sub, SG_CHUNK):
            rs = slice(r0, r0 + SG_CHUNK)
            sv = _dot(wg, vb_ref[rs, cs]) + bg
            gt_ref[rs, cs] = (u_ref[rs, cs].astype(F32) * sv).astype(BF16)

    def project_out(s0):
        ss = slice(s0, s0 + sub)
        o_ref[ss, :] = x_ref[ss, :] + _dot(gt_ref[ss, :], wout_ref[...])

    chunks = list(range(0, 2 * SG_WIDTH, tn))
    h_ref[...] = _rms(x_ref[...], g_ref[...]).astype(BF16)
    for c in chunks:
        project(0, c)
    for i, c in enumerate(chunks):
        project(sub, c)
        layernorm(i * slab)
    for g in range(SG_GROUPS):
        gate(0, g)
        layernorm(sub + g * (sub // SG_GROUPS))
    project_out(0)
    for g in range(SG_GROUPS):
        gate(sub, g)
    project_out(sub)


def _gmlp_layer(x2, gain, w_in, vnorm_g, vnorm_b, w_s, b_s, w_out, tm=1024, tn=512, sub=512):
    n, d = x2.shape
    return pl.pallas_call(
        functools.partial(_gmlp_kernel, tm=tm, tn=tn, sub=sub),
        grid=(n // tm,),
        in_specs=[pl.BlockSpec((tm, d), lambda i: (i, 0)),
                  _resident((1, d)),
                  _resident((d, 2 * SG_WIDTH)),
                  _resident((1, SG_WIDTH)),
                  _resident((1, SG_WIDTH)),
                  _resident((SG_GROUPS, SG_CHUNK, SG_CHUNK)),
                  _resident((SG_CHUNK, SG_GROUPS)),
                  _resident((SG_WIDTH, d))],
        out_specs=pl.BlockSpec((tm, d), lambda i: (i, 0)),
        out_shape=jax.ShapeDtypeStruct((n, d), F32),
        scratch_shapes=[pltpu.VMEM((tm, d), BF16),
                        pltpu.VMEM((tm, SG_WIDTH), BF16),
                        pltpu.VMEM((tm, SG_WIDTH), F32),
                        pltpu.VMEM((tm, SG_WIDTH), BF16),
                        pltpu.VMEM((tm, SG_WIDTH), BF16)],
        compiler_params=_params(("parallel",)),
        name="gmlp_mixer",
    )(x2, gain.reshape(1, d), w_in.astype(BF16), vnorm_g.reshape(1, -1), vnorm_b.reshape(1, -1),
      w_s, b_s.T, w_out.astype(BF16))


def _ssd_kernel(z_ref, xbc_ref, dt_ref, cw_ref, cb_ref, dtb_ref, alog_ref, dsk_ref, ng_ref,
                tril_ref, e_ref, o_ref, xpad_ref, st_ref, xs_ref, b_ref, c_ref, *, cw):
    L = SSM_CHUNK
    T = z_ref.shape[0]
    hist = 8

    @pl.when(pl.program_id(1) == 0)
    def _():
        xpad_ref[0:hist, :] = jnp.zeros((hist, SSM_CONV_CH), F32)
        st_ref[...] = jnp.zeros_like(st_ref)

    for j in range(0, SSM_CONV_CH, cw):
        js = slice(j, j + cw)
        xpad_ref[hist:hist + T, js] = xbc_ref[:, js].astype(F32)
        acc = cb_ref[:, js] + cw_ref[SSM_CONV - 1:SSM_CONV, js] * xpad_ref[hist:hist + T, js]
        for k in range(1, SSM_CONV):
            acc = acc + cw_ref[SSM_CONV - 1 - k:SSM_CONV - k, js] * xpad_ref[hist - k:hist - k + T, js]
        xpad_ref[0:hist, js] = xpad_ref[T:T + hist, js]
        act = _silu(acc)
        if j < SSM_D_INNER:
            xs_ref[:, js] = act
        elif j < SSM_D_INNER + SSM_GN:
            b_ref[:, j - SSM_D_INNER:j - SSM_D_INNER + cw] = act.astype(BF16)
        else:
            o0 = j - SSM_D_INNER - SSM_GN
            c_ref[:, o0:o0 + cw] = act.astype(BF16)

    tril = tril_ref[...]
    row = lax.broadcasted_iota(jnp.int32, (L, L), 0)
    col = lax.broadcasted_iota(jnp.int32, (L, L), 1)
    causal = col <= row
    neg_a = -jnp.exp(alog_ref[...])

    for r0 in range(0, T, L):
        rs = slice(r0, r0 + L)
        dtr = dt_ref[rs, :] + dtb_ref[...]
        dt = jnp.maximum(dtr, 0.0) + jnp.log1p(jnp.exp(-jnp.abs(dtr)))
        da = dt * neg_a
        cs = sum(_dot(tril, p) for p in _split3(da)) * LOG2E
        cs_t = cs.T
        dt3 = _split3(dt)
        cs3 = _split3(cs)

        for g in range(SSM_GROUPS):
            sl = slice(g * SSM_GW, (g + 1) * SSM_GW)
            eg = e_ref[:, sl]
            dt_e = sum(_dot(p, eg) for p in dt3)
            cs_e = sum(_dot(p, eg) for p in cs3)
            xs_g = xs_ref[rs, sl]
            xdt = xs_g * dt_e
            xdt_b = xdt.astype(BF16)
            cs_last = cs_e[L - 1:L, :]
            xw_b = (xdt * jnp.exp2(cs_last - cs_e)).astype(BF16)
            bg = b_ref[rs, g * SSM_STATE:(g + 1) * SSM_STATE]
            cg = c_ref[rs, g * SSM_STATE:(g + 1) * SSM_STATE]
            cb = _dot_nt(cg, bg)
            st_prev = st_ref[g]
            y_off = _dot(cg, st_prev.astype(BF16)) * jnp.exp2(cs_e)
            st_ref[g] = st_prev * jnp.exp2(cs_last) + _dot_tn(bg, xw_b)
            ys = []
            for r in range(SSM_HPG):
                h = g * SSM_HPG + r
                seg = cs[:, h:h + 1] - cs_t[h:h + 1, :]
                dec = jnp.exp2(jnp.where(causal, seg, NEG_INF))
                m = (cb * dec).astype(BF16)
                ys.append(_dot(m, xdt_b[:, r * SSM_HEAD_DIM:(r + 1) * SSM_HEAD_DIM]))
            y = jnp.concatenate(ys, axis=1) + y_off + dsk_ref[:, sl] * xs_g
            yz = y * _silu(z_ref[rs, sl].astype(F32))
            ms = jnp.mean(yz * yz, axis=-1, keepdims=True)
            o_ref[rs, sl] = (yz * lax.rsqrt(ms + NORM_EPS) * ng_ref[:, sl]).astype(BF16)


def _ssd_core(z, xbc, dtp, conv_w, conv_b, dt_bias, a_log, d_skip, norm_g, batch, seq):
    L = SSM_CHUNK
    T = SSM_SUB * L
    nc = seq // T
    tril = jnp.asarray(np.tril(np.ones((L, L), np.float32)), BF16)
    e_np = np.zeros((LANE, SSM_D_INNER), np.float32)
    for h in range(SSM_HEADS):
        e_np[h, h * SSM_HEAD_DIM:(h + 1) * SSM_HEAD_DIM] = 1.0
    expand = jnp.asarray(e_np, BF16)
    pad = LANE - SSM_HEADS
    dtb = jnp.pad(dt_bias.astype(F32), (0, pad)).reshape(1, LANE)
    alog = jnp.pad(a_log.astype(F32), (0, pad)).reshape(1, LANE)
    dsk = jnp.repeat(d_skip.astype(F32), SSM_HEAD_DIM).reshape(1, SSM_D_INNER)
    row = lambda b, c: (b * nc + c, 0)
    return pl.pallas_call(
        functools.partial(_ssd_kernel, cw=512),
        grid=(batch, nc),
        in_specs=[pl.BlockSpec((T, SSM_D_INNER), row),
                  pl.BlockSpec((T, SSM_CONV_CH), row),
                  pl.BlockSpec((T, LANE), row),
                  _resident((SSM_CONV, SSM_CONV_CH)),
                  _resident((1, SSM_CONV_CH)),
                  _resident((1, LANE)),
                  _resident((1, LANE)),
                  _resident((1, SSM_D_INNER)),
                  _resident((1, SSM_D_INNER)),
                  _resident((L, L)),
                  _resident((LANE, SSM_D_INNER))],
        out_specs=pl.BlockSpec((T, SSM_D_INNER), row),
        out_shape=jax.ShapeDtypeStruct((batch * seq, SSM_D_INNER), BF16),
        scratch_shapes=[pltpu.VMEM((T + 8, SSM_CONV_CH), F32),
                        pltpu.VMEM((SSM_GROUPS, SSM_STATE, SSM_GW), F32),
                        pltpu.VMEM((T, SSM_D_INNER), F32),
                        pltpu.VMEM((T, SSM_GN), BF16),
                        pltpu.VMEM((T, SSM_GN), BF16)],
        compiler_params=_params(("arbitrary", "arbitrary")),
        name="ssd_core",
    )(z, xbc, dtp, conv_w, conv_b.reshape(1, -1), dtb, alog, dsk, norm_g.reshape(1, -1), tril, expand)


def _mamba_layer(x2, gain, w_in, conv_w, conv_b, dt_bias, a_log, d_skip, norm_g, w_out, batch, seq):
    d = x2.shape[1]
    nzx = SSM_D_INNER + SSM_CONV_CH
    w_dt = jnp.pad(w_in[:, nzx:], ((0, 0), (0, LANE - SSM_HEADS)))
    w_cat = jnp.concatenate([w_in[:, :nzx], w_dt], axis=1).astype(BF16)
    z, xbc, dtp = _norm_proj(x2, gain, w_cat,
                             [(SSM_D_INNER, BF16, 1.0), (SSM_CONV_CH, BF16, 1.0), (LANE, F32, 1.0)])
    yn = _ssd_core(z, xbc, dtp, conv_w, conv_b, dt_bias, a_log, d_skip, norm_g, batch, seq)
    return yn, w_out.astype(BF16)


def _cmp_kernel(x_ref, w1_ref, w2_ref, pos_ref, o_ref):
    DH = NSA_DH
    nblk = x_ref.shape[0] // CMP_STRIDE
    npair = x_ref.shape[1] // DH
    posb = _dot(jnp.broadcast_to(pos_ref[...], (8, CMP_BLOCK * DH)).astype(BF16), w1_ref[...])[0:1, :]
    a = [jnp.zeros((nblk, CMP_HIDDEN), F32) for _ in range(npair)]
    b = [jnp.zeros((nblk, CMP_HIDDEN), F32) for _ in range(npair)]
    for l in range(CMP_STRIDE):
        xl = x_ref[pl.ds(l, nblk, stride=CMP_STRIDE), :].astype(BF16)
        for g in range(npair):
            xg = xl[:, g * DH:(g + 1) * DH]
            a[g] = a[g] + _dot(xg, w1_ref[l * DH:(l + 1) * DH, :])
            b[g] = b[g] + _dot(xg, w1_ref[(CMP_STRIDE + l) * DH:(CMP_STRIDE + l + 1) * DH, :])
    for g in range(npair):
        pre = a[g] + pltpu.roll(b[g], nblk - 1, 0) + posb
        o_ref[g] = _dot(_gelu(pre).astype(BF16), w2_ref[...]).astype(o_ref.dtype)


def _nsa_compress(kcv, w1, w2, pos, batch, seq):
    nblk = seq // CMP_STRIDE
    npair = LANE // NSA_DH
    slabs = NSA_KV // npair
    return pl.pallas_call(
        _cmp_kernel,
        grid=(batch, 2 * slabs),
        in_specs=[pl.BlockSpec((seq, LANE), lambda b, j: (b, j)),
                  pl.BlockSpec((None,) + w1.shape[1:], lambda b, j: (j // slabs, 0, 0)),
                  pl.BlockSpec((None,) + w2.shape[1:], lambda b, j: (j // slabs, 0, 0)),
                  pl.BlockSpec((None,) + pos.shape[1:], lambda b, j: (j // slabs, 0, 0))],
        out_specs=pl.BlockSpec((None, None, npair, nblk, NSA_DH), lambda b, j: (j // slabs, b, j % slabs, 0, 0)),
        out_shape=jax.ShapeDtypeStruct((2, batch, NSA_KV, nblk, NSA_DH), BF16),
        compiler_params=_params(("parallel", "parallel")),
        name="nsa_compress",
    )(kcv, w1, w2, pos)


def _nsa_attn_kernel(q_ref, gl_ref, kv_ref, kc_ref, vc_ref, ovt_ref, ind_ref,
                     o_ref, ksp_ref, kws_ref, vst_ref, vwt_ref, vct_ref, m_ref, acc_ref):
    R, TQ, TK, DH, NG = NSA_REP, NSA_TQ, NSA_TK, NSA_DH, NSA_KV
    W = R * TQ
    qi = pl.program_id(1)
    t0 = pl.multiple_of(qi * TQ, TQ)
    seq = kv_ref.shape[0]
    ncp = kc_ref.shape[1]
    nsel = seq // SEL_BLOCK
    groups = range(NG)

    @pl.when(qi == 0)
    def _():
        def part(kind, a):
            c0 = (kind * NG + a) * DH
            return kv_ref[:, c0:c0 + DH]

        def t_ones(v):
            n = v.shape[0]
            one = jnp.where(lax.broadcasted_iota(jnp.int32, (n, LANE - DH), 1) == 0, 1.0, 0.0)
            vp = jnp.concatenate([v.astype(F32), one], axis=1)
            vt = jnp.concatenate([vp[j:j + LANE].T for j in range(0, n, LANE)], axis=1)
            return vt[0:NSA_VROWS].astype(BF16)
        for a in groups:
            ksp_ref[a] = jnp.concatenate([part(0, a), ind_ref[...]], axis=1)
            kws_ref[a] = part(2, a)
            vst_ref[a] = t_ones(part(1, a))
            vwt_ref[a] = t_ones(part(3, a))
            vct_ref[a] = t_ones(vc_ref[a])

    q_heads = [[q_ref[:, (a * R + r) * DH:(a * R + r + 1) * DH] for r in range(R)] for a in groups]
    qs = [jnp.concatenate(q_heads[a], axis=0) for a in groups]
    t_lane = t0 + lax.broadcasted_iota(jnp.int32, (1, TQ), 1)

    def heads(x):
        return jnp.concatenate([x] * R, axis=1)

    def colmax(s):
        return jnp.max(s, axis=0, keepdims=True)

    k0w = pl.multiple_of(jnp.maximum(t0 - WINDOW, 0), TQ)
    kpos = k0w + lax.broadcasted_iota(jnp.int32, (NSA_WSPAN, TQ), 0)
    bias_w = heads(jnp.where((kpos <= t_lane) & (kpos > t_lane - WINDOW), 0.0, NEG_INF))
    sw = [_dot_nt(kws_ref[a, pl.ds(k0w, NSA_WSPAN), :], qs[a]) + bias_w for a in groups]
    aw = [_dot(vwt_ref[a, :, pl.ds(k0w, NSA_WSPAN)], jnp.exp2(sw[a] - colmax(sw[a])).astype(BF16)) for a in groups]

    n_sub = lax.broadcasted_iota(jnp.int32, (ncp, TQ), 0)
    valid_c = (n_sub * CMP_STRIDE + (CMP_BLOCK - 1)) <= t_lane
    bias_c = heads(jnp.where(valid_c, 0.0, NEG_INF))
    keep_c = heads(jnp.where(valid_c, 1.0, 0.0))
    ovt = ovt_ref[...]
    sc = [_dot_nt(kc_ref[a], qs[a]) + bias_c for a in groups]
    oc, imp_t = [], []
    for a in groups:
        e = jnp.exp2(sc[a] - colmax(sc[a])) * keep_c
        hi, lo = _split3(e)[:2]
        both = _dot(jnp.concatenate([vct_ref[a], ovt], axis=0), hi)
        u = both[NSA_VROWS:, :] + _dot(ovt, lo)
        l_t = u[nsel:nsel + 1, :]
        un = u[0:nsel, :] / jnp.where(l_t > 0.0, l_t, 1.0)
        imp = un[:, 0:TQ]
        for r in range(1, R):
            imp = imp + un[:, r * TQ:(r + 1) * TQ]
        imp_t.append(imp)
        oc.append(both[0:NSA_VROWS, :])

    j_i = lax.broadcasted_iota(jnp.int32, (nsel, TQ), 0)
    t_l = t0 + lax.broadcasted_iota(jnp.int32, (nsel, TQ), 1)
    cur = lax.shift_right_logical(t_l, 6)
    forced = (j_i == 0) | (j_i == cur) | (j_i == cur - 1)
    future = j_i * SEL_BLOCK > t_l
    val = [jnp.where(forced, FORCE, jnp.where(future, -FORCE, imp_t[a])) for a in groups]
    cnt = [jnp.zeros((nsel, TQ), F32) for _ in groups]
    for i in range(nsel):
        for a in groups:
            vi = val[a][i:i + 1, :]
            ge = jnp.where(vi >= val[a], 1.0, 0.0)
            gt = jnp.where(vi > val[a], 1.0, 0.0)
            cnt[a] = cnt[a] + jnp.where(j_i > i, ge, gt)
    in_loop = j_i < qi * (TQ // SEL_BLOCK)
    qsp = []
    for a in groups:
        sb_t = jnp.where((cnt[a] < float(SEL_TOP_N)) & in_loop, 0.0, NEG_INF)
        sb = jnp.concatenate([sb_t, jnp.zeros((LANE - nsel, TQ), F32)], axis=0).T
        sbb = sb[:, :DH].astype(BF16)
        qsp.append(jnp.concatenate([jnp.concatenate([qh, sbb], axis=1) for qh in q_heads[a]], axis=0))

    k_sub = lax.broadcasted_iota(jnp.int32, (TQ, TQ), 0)
    i_lane = lax.broadcasted_iota(jnp.int32, (TQ, TQ), 1)
    causal_b = heads(jnp.where(k_sub <= i_lane, 0.0, NEG_INF))
    sd = [_dot_nt(ksp_ref[a, pl.ds(t0, TQ), 0:DH], qs[a]) + causal_b for a in groups]
    for a in groups:
        m = colmax(sd[a])
        m_ref[a] = m
        acc_ref[a] = _dot(vst_ref[a, :, pl.ds(t0, TQ)], jnp.exp2(sd[a] - m).astype(BF16))

    gates_t = [jax.nn.sigmoid(gl_ref[:, a * LANE:(a + 1) * LANE]).T for a in groups]

    def gate_row(a, b):
        return jnp.concatenate([gates_t[a][3 * r + b:3 * r + b + 1, :] for r in range(R)], axis=1)

    part = []
    for a in groups:
        l_c = oc[a][DH:DH + 1, :]
        w_c = gate_row(a, 0) / jnp.where(l_c > 0.0, l_c, 1.0)
        w_w = gate_row(a, 2) / aw[a][DH:DH + 1, :]
        part.append(w_c * oc[a][0:DH, :] + w_w * aw[a][0:DH, :])

    def sel_tile(kt, carry):
        k0 = pl.multiple_of(kt * TK, TK)
        s_all = [_dot_nt(ksp_ref[a, pl.ds(k0, TK), :], qsp[a]) for a in groups]
        for a in groups:
            m_old = m_ref[a]
            m_new = jnp.maximum(m_old, colmax(s_all[a]))
            m_ref[a] = m_new
            pv = _dot(vst_ref[a, :, pl.ds(k0, TK)], jnp.exp2(s_all[a] - m_new).astype(BF16))
            acc_ref[a] = jnp.exp2(m_old - m_new) * acc_ref[a] + pv
        return carry

    tq_per_tk = TK // TQ
    lax.fori_loop(0, lax.shift_right_logical(qi + tq_per_tk - 1, tq_per_tk.bit_length() - 1), sel_tile, 0)

    cols = []
    for a in groups:
        acc = acc_ref[a]
        o_t = part[a] + (gate_row(a, 1) / acc[DH:DH + 1, :]) * acc[0:DH, :]
        for r in range(0, R, 2):
            pair = jnp.concatenate([o_t[:, r * TQ:(r + 1) * TQ], o_t[:, (r + 1) * TQ:(r + 2) * TQ]], axis=0)
            cols.append(pair.T)
    o_ref[...] = jnp.concatenate(cols, axis=1).astype(o_ref.dtype)


def _nsa_attention(q, gl, kv4, cmp, batch, seq):
    R, TQ, DH, G = NSA_REP, NSA_TQ, NSA_DH, NSA_KV
    nq = seq // TQ
    ncp = cmp.shape[3]
    nsel = seq // SEL_BLOCK
    nc = (seq - CMP_BLOCK) // CMP_STRIDE + 1
    assert ncp == LANE and nsel < DH and NSA_TK % LANE == 0 and TQ % SEL_BLOCK == 0 and 2 * DH == LANE and R % 2 == 0
    ov = np.zeros((DH, ncp), np.float32)
    for n in range(nc):
        for j in range(nsel):
            if n * CMP_STRIDE <= j * SEL_BLOCK + SEL_BLOCK - 1 and n * CMP_STRIDE + CMP_BLOCK - 1 >= j * SEL_BLOCK:
                ov[j, n] = 1.0
    ov[nsel, :] = 1.0
    ind = np.zeros((seq, DH), np.float32)
    for j in range(nsel):
        ind[j * SEL_BLOCK:(j + 1) * SEL_BLOCK, j] = 1.0
    cmp_spec = lambda idx: pl.BlockSpec((None, None, G, ncp, DH), lambda b, i: (idx, b, 0, 0, 0))
    return pl.pallas_call(
        _nsa_attn_kernel,
        grid=(batch, nq),
        in_specs=[pl.BlockSpec((TQ, G * R * DH), lambda b, i: (b * nq + i, 0)),
                  pl.BlockSpec((TQ, G * LANE), lambda b, i: (b * nq + i, 0)),
                  pl.BlockSpec((seq, 4 * G * DH), lambda b, i: (b, 0)),
                  cmp_spec(0), cmp_spec(1),
                  _resident((DH, ncp)),
                  _resident((seq, DH))],
        out_specs=pl.BlockSpec((TQ, G * R * DH), lambda b, i: (b * nq + i, 0)),
        out_shape=jax.ShapeDtypeStruct((batch * seq, NSA_HEADS * DH), BF16),
        scratch_shapes=[pltpu.VMEM((G, seq, 2 * DH), BF16),
                        pltpu.VMEM((G, seq, DH), BF16),
                        pltpu.VMEM((G, NSA_VROWS, seq), BF16),
                        pltpu.VMEM((G, NSA_VROWS, seq), BF16),
                        pltpu.VMEM((G, NSA_VROWS, ncp), BF16),
                        pltpu.VMEM((G, 1, R * TQ), F32),
                        pltpu.VMEM((G, NSA_VROWS, R * TQ), F32)],
        compiler_params=_params(("arbitrary", "arbitrary")),
        name="nsa_attention",
    )(q, gl, kv4, cmp, cmp, jnp.asarray(ov, BF16), jnp.asarray(ind, BF16))


def _nsa_layer(x2, gain, w_in, pos_k, pos_v, k_w1, k_w2, v_w1, v_w2, w_out, batch, seq):
    d = x2.shape[1]
    nq = NSA_HEADS * NSA_DH
    nkv = 6 * NSA_KV * NSA_DH
    ngate = 3 * NSA_REP
    w_gate = w_in[:, nq + nkv:].reshape(d, NSA_KV, ngate)
    w_gate = jnp.pad(w_gate, ((0, 0), (0, 0), (0, LANE - ngate))).reshape(d, NSA_KV * LANE)
    w_cat = jnp.concatenate([w_in[:, :nq + nkv], w_gate], axis=1).astype(BF16)
    ncv = 2 * NSA_KV * NSA_DH
    q, kcv, kv4, gl = _norm_proj(x2, gain, w_cat,
                                 [(nq, BF16, NSA_DH ** -0.5 * LOG2E), (ncv, F32, 1.0), (nkv - ncv, BF16, 1.0),
                                  (NSA_KV * LANE, F32, 1.0)])
    cmp = _nsa_compress(kcv,
                        jnp.stack([k_w1, v_w1]).astype(BF16),
                        jnp.stack([k_w2, v_w2]).astype(BF16),
                        jnp.stack([pos_k.reshape(1, -1), pos_v.reshape(1, -1)]), batch, seq)
    o = _nsa_attention(q, gl, kv4, cmp, batch, seq)
    return o, w_out.astype(BF16)


def kernel(x, norm_gains, final_norm, ff_w1, ff_w2, l0_sg_w_in, l0_sg_vnorm_g, l0_sg_vnorm_b, l0_sg_w_s, l0_sg_b_s, l0_sg_w_out, l1_ssm_w_in, l1_ssm_conv_w, l1_ssm_conv_b, l1_ssm_dt_bias, l1_ssm_a_log, l1_ssm_d_skip, l1_ssm_norm_g, l1_ssm_w_out, l2_nsa_w_in, l2_nsa_cmp_pos_k, l2_nsa_cmp_pos_v, l2_nsa_cmp_k_w1, l2_nsa_cmp_k_w2, l2_nsa_cmp_v_w1, l2_nsa_cmp_v_w2, l2_nsa_w_out, l3_sg_w_in, l3_sg_vnorm_g, l3_sg_vnorm_b, l3_sg_w_s, l3_sg_b_s, l3_sg_w_out):
    batch, seq, d = x.shape
    x2 = x.reshape(batch * seq, d)
    w1 = ff_w1.astype(BF16)
    w2 = ff_w2.astype(BF16)

    x2 = _gmlp_layer(x2, norm_gains[0, 0], l0_sg_w_in, l0_sg_vnorm_g, l0_sg_vnorm_b, l0_sg_w_s, l0_sg_b_s, l0_sg_w_out)
    x2 = _mlp(x2, norm_gains[0, 1], w1[0], w2[0])
    mix = _mamba_layer(x2, norm_gains[1, 0], l1_ssm_w_in, l1_ssm_conv_w, l1_ssm_conv_b, l1_ssm_dt_bias,
                       l1_ssm_a_log, l1_ssm_d_skip, l1_ssm_norm_g, l1_ssm_w_out, batch, seq)
    x2 = _mlp(x2, norm_gains[1, 1], w1[1], w2[1], mixer_proj=mix)
    mix = _nsa_layer(x2, norm_gains[2, 0], l2_nsa_w_in, l2_nsa_cmp_pos_k, l2_nsa_cmp_pos_v, l2_nsa_cmp_k_w1,
                     l2_nsa_cmp_k_w2, l2_nsa_cmp_v_w1, l2_nsa_cmp_v_w2, l2_nsa_w_out, batch, seq)
    x2 = _mlp(x2, norm_gains[2, 1], w1[2], w2[2], mixer_proj=mix)
    x2 = _gmlp_layer(x2, norm_gains[3, 0], l3_sg_w_in, l3_sg_vnorm_g, l3_sg_vnorm_b, l3_sg_w_s, l3_sg_b_s, l3_sg_w_out)
    x2 = _mlp(x2, norm_gains[3, 1], w1[3], w2[3], final_gain=final_norm)
    return x2.reshape(batch, seq, d)
```

```python
import functools

import numpy as np
import jax
import jax.numpy as jnp
from jax import lax
from jax.experimental import pallas as pl
from jax.experimental.pallas import tpu as pltpu

F32 = jnp.float32
BF16 = jnp.bfloat16

D_MODEL = 1024
D_FF = 4 * D_MODEL
NORM_EPS = 1e-5
NEG_INF = -1e30
FORCE = 1e6
SQRT_HALF = 0.7071067811865476
LOG2E = 1.4426950408889634

SG_CHUNK = 128
SG_WIDTH = 2 * D_MODEL
SG_GROUPS = 8
SG_GW = SG_WIDTH // SG_GROUPS

SSM_D_INNER = 2 * D_MODEL
SSM_HEAD_DIM = 64
SSM_HEADS = SSM_D_INNER // SSM_HEAD_DIM
SSM_GROUPS = 8
SSM_STATE = 128
SSM_CONV = 4
SSM_CHUNK = 128
SSM_SUB = 4
SSM_GN = SSM_GROUPS * SSM_STATE
SSM_CONV_CH = SSM_D_INNER + 2 * SSM_GN
SSM_GW = SSM_D_INNER // SSM_GROUPS
SSM_HPG = SSM_HEADS // SSM_GROUPS

NSA_HEADS = 16
NSA_KV = 4
NSA_DH = 64
NSA_REP = NSA_HEADS // NSA_KV
CMP_BLOCK = 32
CMP_STRIDE = 16
CMP_HIDDEN = 4 * NSA_DH
SEL_BLOCK = 64
SEL_TOP_N = 16
WINDOW = 512
NSA_TQ = 128
NSA_TK = 256
NSA_WSPAN = WINDOW + NSA_TQ
NSA_VROWS = NSA_DH + 16

LANE = 128
VMEM_LIMIT = 56 * 1024 * 1024


def _params(sem):
    return pltpu.CompilerParams(dimension_semantics=sem, vmem_limit_bytes=VMEM_LIMIT)


def _resident(shape):
    nd = len(shape)
    return pl.BlockSpec(shape, lambda *_: (0,) * nd, pipeline_mode=pl.Buffered(1))


def _rms(x, g):
    return x * lax.rsqrt(jnp.mean(x * x, axis=-1, keepdims=True) + NORM_EPS) * g


def _gelu(a):
    return 0.5 * a * (1.0 + lax.erf(a * SQRT_HALF))


def _silu(a):
    return a * jax.nn.sigmoid(a)


def _dot(a, b):
    return jnp.dot(a, b, preferred_element_type=F32)


def _dot_nt(a, b):
    return lax.dot_general(a, b, (((1,), (1,)), ((), ())), preferred_element_type=F32)


def _dot_tn(a, b):
    return lax.dot_general(a, b, (((0,), (0,)), ((), ())), preferred_element_type=F32)


def _split3(v):
    hi = v.astype(BF16)
    r = v - hi.astype(F32)
    mid = r.astype(BF16)
    lo = (r - mid.astype(F32)).astype(BF16)
    return hi, mid, lo


def _norm_proj_kernel(x_ref, g_ref, w_ref, *out_refs, outs, tn):
    h = _rms(x_ref[...], g_ref[...]).astype(BF16)
    c0 = 0
    for o_ref, (width, _, scale) in zip(out_refs, outs):
        for c in range(0, width, tn):
            ce = min(c + tn, width)
            a = _dot(h, w_ref[:, c0 + c:c0 + ce])
            if scale != 1.0:
                a = a * scale
            o_ref[:, c:ce] = a.astype(o_ref.dtype)
        c0 += width


def _norm_proj(x2, gain, w, outs, tm=512, tn=512):
    n, d = x2.shape
    wtot = sum(o[0] for o in outs)
    assert w.shape == (d, wtot)
    return pl.pallas_call(
        functools.partial(_norm_proj_kernel, outs=tuple(outs), tn=tn),
        grid=(n // tm,),
        in_specs=[pl.BlockSpec((tm, d), lambda i: (i, 0)),
                  _resident((1, d)),
                  _resident((d, wtot))],
        out_specs=[pl.BlockSpec((tm, o[0]), lambda i: (i, 0)) for o in outs],
        out_shape=[jax.ShapeDtypeStruct((n, o[0]), o[1]) for o in outs],
        compiler_params=_params(("parallel",)),
        name="norm_proj",
    )(x2, gain.reshape(1, d), w)


def _mlp_kernel(*refs, tf, final, mixer_proj):
    if mixer_proj:
        a_ref, wo_ref, x_ref, g_ref, w1_ref, w2_ref, fg_ref, o_ref, h_ref, acc_ref = refs
        x = x_ref[...] + _dot(a_ref[...], wo_ref[...])
    else:
        x_ref, g_ref, w1_ref, w2_ref, fg_ref, o_ref, h_ref, acc_ref = refs
        x = x_ref[...]
    h_ref[...] = _rms(x, g_ref[...]).astype(BF16)
    acc_ref[...] = x
    for c in range(0, D_FF, tf):
        a = _dot(h_ref[...], w1_ref[:, c:c + tf])
        a = jnp.square(jnp.maximum(a, 0.0)).astype(BF16)
        acc_ref[...] += _dot(a, w2_ref[c:c + tf, :])
    y = acc_ref[...]
    if final:
        y = _rms(y, fg_ref[...])
    o_ref[...] = y


def _mlp(x2, gain, w1, w2, final_gain=None, mixer_proj=None, tm=512, tf=512):
    n, d = x2.shape
    final = final_gain is not None
    fg = (final_gain if final else gain).reshape(1, d)
    in_specs = [pl.BlockSpec((tm, d), lambda i: (i, 0)),
                _resident((1, d)),
                _resident((d, D_FF)),
                _resident((D_FF, d)),
                _resident((1, d))]
    args = (x2, gain.reshape(1, d), w1, w2, fg)
    if mixer_proj is not None:
        a, w_out = mixer_proj
        k = a.shape[1]
        in_specs = [pl.BlockSpec((tm, k), lambda i: (i, 0)), _resident((k, d))] + in_specs
        args = (a, w_out) + args
    return pl.pallas_call(
        functools.partial(_mlp_kernel, tf=tf, final=final, mixer_proj=mixer_proj is not None),
        grid=(n // tm,),
        in_specs=in_specs,
        out_specs=pl.BlockSpec((tm, d), lambda i: (i, 0)),
        out_shape=jax.ShapeDtypeStruct((n, d), F32),
        scratch_shapes=[pltpu.VMEM((tm, d), BF16), pltpu.VMEM((tm, d), F32)],
        compiler_params=_params(("parallel",)),
        name="sqrelu_mlp",
    )(*args)


def _gmlp_kernel(x_ref, g_ref, win_ref, lng_ref, lnb_ref, ws_ref, bs_ref, wout_ref, o_ref,
                 h_ref, u_ref, v_ref, vb_ref, gt_ref, *, tm, tn, sub):
    assert tm == 2 * sub
    slab = sub // (2 * SG_WIDTH // tn)
    row = lax.broadcasted_iota(jnp.int32, (SG_CHUNK, SG_CHUNK), 0)
    col = lax.broadcasted_iota(jnp.int32, (SG_CHUNK, SG_CHUNK), 1)
    causal = col <= row

    def project(s0, c):
        ss = slice(s0, s0 + sub)
        a = _gelu(_dot(h_ref[ss, :], win_ref[:, c:c + tn]))
        if c < SG_WIDTH:
            u_ref[ss, c:c + tn] = a.astype(BF16)
        else:
            v_ref[ss, c - SG_WIDTH:c - SG_WIDTH + tn] = a

    def layernorm(r0):
        rs = slice(r0, r0 + slab)
        v = v_ref[rs, :]
        mu = jnp.mean(v, axis=-1, keepdims=True)
        dv = v - mu
        var = jnp.mean(dv * dv, axis=-1, keepdims=True)
        vb_ref[rs, :] = (dv * lax.rsqrt(var + NORM_EPS) * lng_ref[...] + lnb_ref[...]).astype(BF16)

    def gate(s0, g):
        wg = jnp.where(causal, ws_ref[g], 0.0).astype(BF16)
        bg = bs_ref[:, g:g + 1]
        cs = slice(g * SG_GW, (g + 1) * SG_GW)
        for r0 in range(s0, s0 + sub, SG_CHUNK):
            rs = slice(r0, r0 + SG_CHUNK)
            sv = _dot(wg, vb_ref[rs, cs]) + bg
            gt_ref[rs, cs] = (u_ref[rs, cs].astype(F32) * sv).astype(BF16)

    def project_out(s0):
        ss = slice(s0, s0 + sub)
        o_ref[ss, :] = x_ref[ss, :] + _dot(gt_ref[ss, :], wout_ref[...])

    chunks = list(range(0, 2 * SG_WIDTH, tn))
    h_ref[...] = _rms(x_ref[...], g_ref[...]).astype(BF16)
    for c in chunks:
        project(0, c)
    for i, c in enumerate(chunks):
        project(sub, c)
        layernorm(i * slab)
    for g in range(SG_GROUPS):
        gate(0, g)
        layernorm(sub + g * (sub // SG_GROUPS))
    project_out(0)
    for g in range(SG_GROUPS):
        gate(sub, g)
    project_out(sub)


def _gmlp_layer(x2, gain, w_in, vnorm_g, vnorm_b, w_s, b_s, w_out, tm=1024, tn=512, sub=512):
    n, d = x2.shape
    return pl.pallas_call(
        functools.partial(_gmlp_kernel, tm=tm, tn=tn, sub=sub),
        grid=(n // tm,),
        in_specs=[pl.BlockSpec((tm, d), lambda i: (i, 0)),
                  _resident((1, d)),
                  _resident((d, 2 * SG_WIDTH)),
                  _resident((1, SG_WIDTH)),
                  _resident((1, SG_WIDTH)),
                  _resident((SG_GROUPS, SG_CHUNK, SG_CHUNK)),
                  _resident((SG_CHUNK, SG_GROUPS)),
                  _resident((SG_WIDTH, d))],
        out_specs=pl.BlockSpec((tm, d), lambda i: (i, 0)),
        out_shape=jax.ShapeDtypeStruct((n, d), F32),
        scratch_shapes=[pltpu.VMEM((tm, d), BF16),
                        pltpu.VMEM((tm, SG_WIDTH), BF16),
                        pltpu.VMEM((tm, SG_WIDTH), F32),
                        pltpu.VMEM((tm, SG_WIDTH), BF16),
                        pltpu.VMEM((tm, SG_WIDTH), BF16)],
        compiler_params=_params(("parallel",)),
        name="gmlp_mixer",
    )(x2, gain.reshape(1, d), w_in.astype(BF16), vnorm_g.reshape(1, -1), vnorm_b.reshape(1, -1),
      w_s, b_s.T, w_out.astype(BF16))


def _ssd_kernel(z_ref, xbc_ref, dt_ref, cw_ref, cb_ref, dtb_ref, alog_ref, dsk_ref, ng_ref,
                tril_ref, e_ref, o_ref, xpad_ref, st_ref, xs_ref, b_ref, c_ref, *, cw):
    L = SSM_CHUNK
    T = z_ref.shape[0]
    hist = 8

    @pl.when(pl.program_id(1) == 0)
    def _():
        xpad_ref[0:hist, :] = jnp.zeros((hist, SSM_CONV_CH), F32)
        st_ref[...] = jnp.zeros_like(st_ref)

    for j in range(0, SSM_CONV_CH, cw):
        js = slice(j, j + cw)
        xpad_ref[hist:hist + T, js] = xbc_ref[:, js].astype(F32)
        acc = cb_ref[:, js] + cw_ref[SSM_CONV - 1:SSM_CONV, js] * xpad_ref[hist:hist + T, js]
        for k in range(1, SSM_CONV):
            acc = acc + cw_ref[SSM_CONV - 1 - k:SSM_CONV - k, js] * xpad_ref[hist - k:hist - k + T, js]
        xpad_ref[0:hist, js] = xpad_ref[T:T + hist, js]
        act = _silu(acc)
        if j < SSM_D_INNER:
            xs_ref[:, js] = act
        elif j < SSM_D_INNER + SSM_GN:
            b_ref[:, j - SSM_D_INNER:j - SSM_D_INNER + cw] = act.astype(BF16)
        else:
            o0 = j - SSM_D_INNER - SSM_GN
            c_ref[:, o0:o0 + cw] = act.astype(BF16)

    tril = tril_ref[...]
    row = lax.broadcasted_iota(jnp.int32, (L, L), 0)
    col = lax.broadcasted_iota(jnp.int32, (L, L), 1)
    causal = col <= row
    neg_a = -jnp.exp(alog_ref[...])

    for r0 in range(0, T, L):
        rs = slice(r0, r0 + L)
        dtr = dt_ref[rs, :] + dtb_ref[...]
        dt = jnp.maximum(dtr, 0.0) + jnp.log1p(jnp.exp(-jnp.abs(dtr)))
        da = dt * neg_a
        cs = sum(_dot(tril, p) for p in _split3(da)) * LOG2E
        cs_t = cs.T
        dt3 = _split3(dt)
        cs3 = _split3(cs)

        for g in range(SSM_GROUPS):
            sl = slice(g * SSM_GW, (g + 1) * SSM_GW)
            eg = e_ref[:, sl]
            dt_e = sum(_dot(p, eg) for p in dt3)
            cs_e = sum(_dot(p, eg) for p in cs3)
            xs_g = xs_ref[rs, sl]
            xdt = xs_g * dt_e
            xdt_b = xdt.astype(BF16)
            cs_last = cs_e[L - 1:L, :]
            xw_b = (xdt * jnp.exp2(cs_last - cs_e)).astype(BF16)
            bg = b_ref[rs, g * SSM_STATE:(g + 1) * SSM_STATE]
            cg = c_ref[rs, g * SSM_STATE:(g + 1) * SSM_STATE]
            cb = _dot_nt(cg, bg)
            st_prev = st_ref[g]
            y_off = _dot(cg, st_prev.astype(BF16)) * jnp.exp2(cs_e)
            st_ref[g] = st_prev * jnp.exp2(cs_last) + _dot_tn(bg, xw_b)
            ys = []
            for r in range(SSM_HPG):
                h = g * SSM_HPG + r
                seg = cs[:, h:h + 1] - cs_t[h:h + 1, :]
                dec = jnp.exp2(jnp.where(causal, seg, NEG_INF))
                m = (cb * dec).astype(BF16)
                ys.append(_dot(m, xdt_b[:, r * SSM_HEAD_DIM:(r + 1) * SSM_HEAD_DIM]))
            y = jnp.concatenate(ys, axis=1) + y_off + dsk_ref[:, sl] * xs_g
            yz = y * _silu(z_ref[rs, sl].astype(F32))
            ms = jnp.mean(yz * yz, axis=-1, keepdims=True)
            o_ref[rs, sl] = (yz * lax.rsqrt(ms + NORM_EPS) * ng_ref[:, sl]).astype(BF16)


def _ssd_core(z, xbc, dtp, conv_w, conv_b, dt_bias, a_log, d_skip, norm_g, batch, seq):
    L = SSM_CHUNK
    T = SSM_SUB * L
    nc = seq // T
    tril = jnp.asarray(np.tril(np.ones((L, L), np.float32)), BF16)
    e_np = np.zeros((LANE, SSM_D_INNER), np.float32)
    for h in range(SSM_HEADS):
        e_np[h, h * SSM_HEAD_DIM:(h + 1) * SSM_HEAD_DIM] = 1.0
    expand = jnp.asarray(e_np, BF16)
    pad = LANE - SSM_HEADS
    dtb = jnp.pad(dt_bias.astype(F32), (0, pad)).reshape(1, LANE)
    alog = jnp.pad(a_log.astype(F32), (0, pad)).reshape(1, LANE)
    dsk = jnp.repeat(d_skip.astype(F32), SSM_HEAD_DIM).reshape(1, SSM_D_INNER)
    row = lambda b, c: (b * nc + c, 0)
    return pl.pallas_call(
        functools.partial(_ssd_kernel, cw=512),
        grid=(batch, nc),
        in_specs=[pl.BlockSpec((T, SSM_D_INNER), row),
                  pl.BlockSpec((T, SSM_CONV_CH), row),
                  pl.BlockSpec((T, LANE), row),
                  _resident((SSM_CONV, SSM_CONV_CH)),
                  _resident((1, SSM_CONV_CH)),
                  _resident((1, LANE)),
                  _resident((1, LANE)),
                  _resident((1, SSM_D_INNER)),
                  _resident((1, SSM_D_INNER)),
                  _resident((L, L)),
                  _resident((LANE, SSM_D_INNER))],
        out_specs=pl.BlockSpec((T, SSM_D_INNER), row),
        out_shape=jax.ShapeDtypeStruct((batch * seq, SSM_D_INNER), BF16),
        scratch_shapes=[pltpu.VMEM((T + 8, SSM_CONV_CH), F32),
                        pltpu.VMEM((SSM_GROUPS, SSM_STATE, SSM_GW), F32),
                        pltpu.VMEM((T, SSM_D_INNER), F32),
                        pltpu.VMEM((T, SSM_GN), BF16),
                        pltpu.VMEM((T, SSM_GN), BF16)],
        compiler_params=_params(("arbitrary", "arbitrary")),
        name="ssd_core",
    )(z, xbc, dtp, conv_w, conv_b.reshape(1, -1), dtb, alog, dsk, norm_g.reshape(1, -1), tril, expand)


def _mamba_layer(x2, gain, w_in, conv_w, conv_b, dt_bias, a_log, d_skip, norm_g, w_out, batch, seq):
    d = x2.shape[1]
    nzx = SSM_D_INNER + SSM_CONV_CH
    w_dt = jnp.pad(w_in[:, nzx:], ((0, 0), (0, LANE - SSM_HEADS)))
    w_cat = jnp.concatenate([w_in[:, :nzx], w_dt], axis=1).astype(BF16)
    z, xbc, dtp = _norm_proj(x2, gain, w_cat,
                             [(SSM_D_INNER, BF16, 1.0), (SSM_CONV_CH, BF16, 1.0), (LANE, F32, 1.0)])
    yn = _ssd_core(z, xbc, dtp, conv_w, conv_b, dt_bias, a_log, d_skip, norm_g, batch, seq)
    return yn, w_out.astype(BF16)


def _cmp_kernel(x_ref, w1_ref, w2_ref, pos_ref, o_ref):
    DH = NSA_DH
    nblk = x_ref.shape[0] // CMP_STRIDE
    npair = x_ref.shape[1] // DH
    posb = _dot(jnp.broadcast_to(pos_ref[...], (8, CMP_BLOCK * DH)).astype(BF16), w1_ref[...])[0:1, :]
    a = [jnp.zeros((nblk, CMP_HIDDEN), F32) for _ in range(npair)]
    b = [jnp.zeros((nblk, CMP_HIDDEN), F32) for _ in range(npair)]
    for l in range(CMP_STRIDE):
        xl = x_ref[pl.ds(l, nblk, stride=CMP_STRIDE), :].astype(BF16)
        for g in range(npair):
            xg = xl[:, g * DH:(g + 1) * DH]
            a[g] = a[g] + _dot(xg, w1_ref[l * DH:(l + 1) * DH, :])
            b[g] = b[g] + _dot(xg, w1_ref[(CMP_STRIDE + l) * DH:(CMP_STRIDE + l + 1) * DH, :])
    for g in range(npair):
        pre = a[g] + pltpu.roll(b[g], nblk - 1, 0) + posb
        o_ref[g] = _dot(_gelu(pre).astype(BF16), w2_ref[...]).astype(o_ref.dtype)


def _nsa_compress(kcv, w1, w2, pos, batch, seq):
    nblk = seq // CMP_STRIDE
    npair = LANE // NSA_DH
    slabs = NSA_KV // npair
    return pl.pallas_call(
        _cmp_kernel,
        grid=(batch, 2 * slabs),
        in_specs=[pl.BlockSpec((seq, LANE), lambda b, j: (b, j)),
                  pl.BlockSpec((None,) + w1.shape[1:], lambda b, j: (j // slabs, 0, 0)),
                  pl.BlockSpec((None,) + w2.shape[1:], lambda b, j: (j // slabs, 0, 0)),
                  pl.BlockSpec((None,) + pos.shape[1:], lambda b, j: (j // slabs, 0, 0))],
        out_specs=pl.BlockSpec((None, None, npair, nblk, NSA_DH), lambda b, j: (j // slabs, b, j % slabs, 0, 0)),
        out_shape=jax.ShapeDtypeStruct((2, batch, NSA_KV, nblk, NSA_DH), BF16),
        compiler_params=_params(("parallel", "parallel")),
        name="nsa_compress",
    )(kcv, w1, w2, pos)


def _nsa_attn_kernel(q_ref, gl_ref, kv_ref, kc_ref, vc_ref, ovt_ref, ind_ref,
                     o_ref, ksp_ref, kws_ref, vst_ref, vwt_ref, vct_ref, m_ref, acc_ref):
    R, TQ, TK, DH, NG = NSA_REP, NSA_TQ, NSA_TK, NSA_DH, NSA_KV
    W = R * TQ
    qi = pl.program_id(1)
    t0 = pl.multiple_of(qi * TQ, TQ)
    seq = kv_ref.shape[0]
    ncp = kc_ref.shape[1]
    nsel = seq // SEL_BLOCK
    groups = range(NG)

    @pl.when(qi == 0)
    def _():
        def part(kind, a):
            c0 = (kind * NG + a) * DH
            return kv_ref[:, c0:c0 + DH]

        def t_ones(v):
            n = v.shape[0]
            one = jnp.where(lax.broadcasted_iota(jnp.int32, (n, LANE - DH), 1) == 0, 1.0, 0.0)
            vp = jnp.concatenate([v, one.astype(BF16)], axis=1)
            eye = jnp.where(lax.broadcasted_iota(jnp.int32, (NSA_VROWS, LANE), 0)
                            == lax.broadcasted_iota(jnp.int32, (NSA_VROWS, LANE), 1), 1.0, 0.0).astype(BF16)
            return _dot_nt(eye, vp).astype(BF16)
        for a in groups:
            ksp_ref[a] = jnp.concatenate([part(0, a), ind_ref[...]], axis=1)
            kws_ref[a] = part(2, a)
            vst_ref[a] = t_ones(part(1, a))
            vwt_ref[a] = t_ones(part(3, a))
            vct_ref[a] = t_ones(vc_ref[a])

    q_heads = [[q_ref[:, (a * R + r) * DH:(a * R + r + 1) * DH] for r in range(R)] for a in groups]
    qs = [jnp.concatenate(q_heads[a], axis=0) for a in groups]
    t_lane = t0 + lax.broadcasted_iota(jnp.int32, (1, TQ), 1)

    def heads(x):
        return jnp.concatenate([x] * R, axis=1)

    def colmax(s):
        return jnp.max(s, axis=0, keepdims=True)

    k0w = pl.multiple_of(jnp.maximum(t0 - WINDOW, 0), TQ)
    kpos = k0w + lax.broadcasted_iota(jnp.int32, (NSA_WSPAN, TQ), 0)
    bias_w = heads(jnp.where((kpos <= t_lane) & (kpos > t_lane - WINDOW), 0.0, NEG_INF))
    sw = [_dot_nt(kws_ref[a, pl.ds(k0w, NSA_WSPAN), :], qs[a]) + bias_w for a in groups]
    aw = [_dot(vwt_ref[a, :, pl.ds(k0w, NSA_WSPAN)], jnp.exp2(sw[a] - colmax(sw[a])).astype(BF16)) for a in groups]

    n_sub = lax.broadcasted_iota(jnp.int32, (ncp, TQ), 0)
    valid_c = (n_sub * CMP_STRIDE + (CMP_BLOCK - 1)) <= t_lane
    bias_c = heads(jnp.where(valid_c, 0.0, NEG_INF))
    keep_c = heads(jnp.where(valid_c, 1.0, 0.0))
    ovt = ovt_ref[...]
    sc = [_dot_nt(kc_ref[a], qs[a]) + bias_c for a in groups]
    oc, imp_t = [], []
    for a in groups:
        e = jnp.exp2(sc[a] - colmax(sc[a])) * keep_c
        hi, lo = _split3(e)[:2]
        both = _dot(jnp.concatenate([vct_ref[a], ovt], axis=0), hi)
        u = both[NSA_VROWS:, :] + _dot(ovt, lo)
        l_t = u[nsel:nsel + 1, :]
        un = u[0:nsel, :] / jnp.where(l_t > 0.0, l_t, 1.0)
        imp = un[:, 0:TQ]
        for r in range(1, R):
            imp = imp + un[:, r * TQ:(r + 1) * TQ]
        imp_t.append(imp)
        oc.append(both[0:NSA_VROWS, :])

    j_i = lax.broadcasted_iota(jnp.int32, (nsel, TQ), 0)
    t_l = t0 + lax.broadcasted_iota(jnp.int32, (nsel, TQ), 1)
    cur = lax.shift_right_logical(t_l, 6)
    forced = (j_i == 0) | (j_i == cur) | (j_i == cur - 1)
    future = j_i * SEL_BLOCK > t_l
    val = [jnp.where(forced, FORCE, jnp.where(future, -FORCE, imp_t[a])) for a in groups]
    cnt = [jnp.zeros((nsel, TQ), F32) for _ in groups]
    for i in range(nsel):
        for a in groups:
            vi = val[a][i:i + 1, :]
            ge = jnp.where(vi >= val[a], 1.0, 0.0)
            gt = jnp.where(vi > val[a], 1.0, 0.0)
            cnt[a] = cnt[a] + jnp.where(j_i > i, ge, gt)
    in_loop = j_i < qi * (TQ // SEL_BLOCK)
    qsp = []
    for a in groups:
        sb_t = jnp.where((cnt[a] < float(SEL_TOP_N)) & in_loop, 0.0, NEG_INF)
        sb = jnp.concatenate([sb_t, jnp.zeros((LANE - nsel, TQ), F32)], axis=0).T
        sbb = sb[:, :DH].astype(BF16)
        qsp.append(jnp.concatenate([jnp.concatenate([qh, sbb], axis=1) for qh in q_heads[a]], axis=0))

    k_sub = lax.broadcasted_iota(jnp.int32, (TQ, TQ), 0)
    i_lane = lax.broadcasted_iota(jnp.int32, (TQ, TQ), 1)
    causal_b = heads(jnp.where(k_sub <= i_lane, 0.0, NEG_INF))
    sd = [_dot_nt(ksp_ref[a, pl.ds(t0, TQ), 0:DH], qs[a]) + causal_b for a in groups]
    for a in groups:
        m = colmax(sd[a])
        m_ref[a] = m
        acc_ref[a] = _dot(vst_ref[a, :, pl.ds(t0, TQ)], jnp.exp2(sd[a] - m).astype(BF16))

    gates_t = [jax.nn.sigmoid(gl_ref[:, a * LANE:(a + 1) * LANE]).T for a in groups]

    def gate_row(a, b):
        return jnp.concatenate([gates_t[a][3 * r + b:3 * r + b + 1, :] for r in range(R)], axis=1)

    part = []
    for a in groups:
        l_c = oc[a][DH:DH + 1, :]
        w_c = gate_row(a, 0) / jnp.where(l_c > 0.0, l_c, 1.0)
        w_w = gate_row(a, 2) / aw[a][DH:DH + 1, :]
        part.append(w_c * oc[a][0:DH, :] + w_w * aw[a][0:DH, :])

    def sel_tile(kt, carry):
        k0 = pl.multiple_of(kt * TK, TK)
        s_all = [_dot_nt(ksp_ref[a, pl.ds(k0, TK), :], qsp[a]) for a in groups]
        for a in groups:
            m_old = m_ref[a]
            m_new = jnp.maximum(m_old, colmax(s_all[a]))
            m_ref[a] = m_new
            pv = _dot(vst_ref[a, :, pl.ds(k0, TK)], jnp.exp2(s_all[a] - m_new).astype(BF16))
            acc_ref[a] = jnp.exp2(m_old - m_new) * acc_ref[a] + pv
        return carry

    tq_per_tk = TK // TQ
    lax.fori_loop(0, lax.shift_right_logical(qi + tq_per_tk - 1, tq_per_tk.bit_length() - 1), sel_tile, 0)

    cols = []
    for a in groups:
        acc = acc_ref[a]
        o_t = part[a] + (gate_row(a, 1) / acc[DH:DH + 1, :]) * acc[0:DH, :]
        for r in range(0, R, 2):
            pair = jnp.concatenate([o_t[:, r * TQ:(r + 1) * TQ], o_t[:, (r + 1) * TQ:(r + 2) * TQ]], axis=0)
            cols.append(pair.T)
    o_ref[...] = jnp.concatenate(cols, axis=1).astype(o_ref.dtype)


def _nsa_attention(q, gl, kv4, cmp, batch, seq):
    R, TQ, DH, G = NSA_REP, NSA_TQ, NSA_DH, NSA_KV
    nq = seq // TQ
    ncp = cmp.shape[3]
    nsel = seq // SEL_BLOCK
    nc = (seq - CMP_BLOCK) // CMP_STRIDE + 1
    assert ncp == LANE and nsel < DH and NSA_TK % LANE == 0 and TQ % SEL_BLOCK == 0 and 2 * DH == LANE and R % 2 == 0
    ov = np.zeros((DH, ncp), np.float32)
    for n in range(nc):
        for j in range(nsel):
            if n * CMP_STRIDE <= j * SEL_BLOCK + SEL_BLOCK - 1 and n * CMP_STRIDE + CMP_BLOCK - 1 >= j * SEL_BLOCK:
                ov[j, n] = 1.0
    ov[nsel, :] = 1.0
    ind = np.zeros((seq, DH), np.float32)
    for j in range(nsel):
        ind[j * SEL_BLOCK:(j + 1) * SEL_BLOCK, j] = 1.0
    cmp_spec = lambda idx: pl.BlockSpec((None, None, G, ncp, DH), lambda b, i: (idx, b, 0, 0, 0))
    return pl.pallas_call(
        _nsa_attn_kernel,
        grid=(batch, nq),
        in_specs=[pl.BlockSpec((TQ, G * R * DH), lambda b, i: (b * nq + i, 0)),
                  pl.BlockSpec((TQ, G * LANE), lambda b, i: (b * nq + i, 0)),
                  pl.BlockSpec((seq, 4 * G * DH), lambda b, i: (b, 0)),
                  cmp_spec(0), cmp_spec(1),
                  _resident((DH, ncp)),
                  _resident((seq, DH))],
        out_specs=pl.BlockSpec((TQ, G * R * DH), lambda b, i: (b * nq + i, 0)),
        out_shape=jax.ShapeDtypeStruct((batch * seq, NSA_HEADS * DH), BF16),
        scratch_shapes=[pltpu.VMEM((G, seq, 2 * DH), BF16),
                        pltpu.VMEM((G, seq, DH), BF16),
                        pltpu.VMEM((G, NSA_VROWS, seq), BF16),
                        pltpu.VMEM((G, NSA_VROWS, seq), BF16),
                        pltpu.VMEM((G, NSA_VROWS, ncp), BF16),
                        pltpu.VMEM((G, 1, R * TQ), F32),
                        pltpu.VMEM((G, NSA_VROWS, R * TQ), F32)],
        compiler_params=_params(("arbitrary", "arbitrary")),
        name="nsa_attention",
    )(q, gl, kv4, cmp, cmp, jnp.asarray(ov, BF16), jnp.asarray(ind, BF16))


def _nsa_layer(x2, gain, w_in, pos_k, pos_v, k_w1, k_w2, v_w1, v_w2, w_out, batch, seq):
    d = x2.shape[1]
    nq = NSA_HEADS * NSA_DH
    nkv = 6 * NSA_KV * NSA_DH
    ngate = 3 * NSA_REP
    w_gate = w_in[:, nq + nkv:].reshape(d, NSA_KV, ngate)
    w_gate = jnp.pad(w_gate, ((0, 0), (0, 0), (0, LANE - ngate))).reshape(d, NSA_KV * LANE)
    w_cat = jnp.concatenate([w_in[:, :nq + nkv], w_gate], axis=1).astype(BF16)
    ncv = 2 * NSA_KV * NSA_DH
    q, kcv, kv4, gl = _norm_proj(x2, gain, w_cat,
                                 [(nq, BF16, NSA_DH ** -0.5 * LOG2E), (ncv, F32, 1.0), (nkv - ncv, BF16, 1.0),
                                  (NSA_KV * LANE, F32, 1.0)])
    cmp = _nsa_compress(kcv,
                        jnp.stack([k_w1, v_w1]).astype(BF16),
                        jnp.stack([k_w2, v_w2]).astype(BF16),
                        jnp.stack([pos_k.reshape(1, -1), pos_v.reshape(1, -1)]), batch, seq)
    o = _nsa_attention(q, gl, kv4, cmp, batch, seq)
    return o, w_out.astype(BF16)


def kernel(x, norm_gains, final_norm, ff_w1, ff_w2, l0_sg_w_in, l0_sg_vnorm_g, l0_sg_vnorm_b, l0_sg_w_s, l0_sg_b_s, l0_sg_w_out, l1_ssm_w_in, l1_ssm_conv_w, l1_ssm_conv_b, l1_ssm_dt_bias, l1_ssm_a_log, l1_ssm_d_skip, l1_ssm_norm_g, l1_ssm_w_out, l2_nsa_w_in, l2_nsa_cmp_pos_k, l2_nsa_cmp_pos_v, l2_nsa_cmp_k_w1, l2_nsa_cmp_k_w2, l2_nsa_cmp_v_w1, l2_nsa_cmp_v_w2, l2_nsa_w_out, l3_sg_w_in, l3_sg_vnorm_g, l3_sg_vnorm_b, l3_sg_w_s, l3_sg_b_s, l3_sg_w_out):
    batch, seq, d = x.shape
    x2 = x.reshape(batch * seq, d)
    w1 = ff_w1.astype(BF16)
    w2 = ff_w2.astype(BF16)

    x2 = _gmlp_layer(x2, norm_gains[0, 0], l0_sg_w_in, l0_sg_vnorm_g, l0_sg_vnorm_b, l0_sg_w_s, l0_sg_b_s, l0_sg_w_out)
    x2 = _mlp(x2, norm_gains[0, 1], w1[0], w2[0])
    mix = _mamba_layer(x2, norm_gains[1, 0], l1_ssm_w_in, l1_ssm_conv_w, l1_ssm_conv_b, l1_ssm_dt_bias,
                       l1_ssm_a_log, l1_ssm_d_skip, l1_ssm_norm_g, l1_ssm_w_out, batch, seq)
    x2 = _mlp(x2, norm_gains[1, 1], w1[1], w2[1], mixer_proj=mix)
    mix = _nsa_layer(x2, norm_gains[2, 0], l2_nsa_w_in, l2_nsa_cmp_pos_k, l2_nsa_cmp_pos_v, l2_nsa_cmp_k_w1,
                     l2_nsa_cmp_k_w2, l2_nsa_cmp_v_w1, l2_nsa_cmp_v_w2, l2_nsa_w_out, batch, seq)
    x2 = _mlp(x2, norm_gains[2, 1], w1[2], w2[2], mixer_proj=mix)
    x2 = _gmlp_layer(x2, norm_gains[3, 0], l3_sg_w_in, l3_sg_vnorm_g, l3_sg_vnorm_b, l3_sg_w_s, l3_sg_b_s, l3_sg_w_out)
    x2 = _mlp(x2, norm_gains[3, 1], w1[3], w2[3], final_gain=final_norm)
    return x2.reshape(batch, seq, d)
```
